```python
import jax, jax.numpy as jnp
from jax import lax
import numpy as np

D_MODEL = 1024
BATCH = 8
SEQ = 4096
DEPTH = 1

N_MEM = 256
SC_WIDTH = D_MODEL
SC_KERNEL = 3
CF_WIDTH = D_MODEL
CF_KERNEL = 31
XA_HEADS = 4
XA_HEAD_DIM = D_MODEL // XA_HEADS
PEER_HEADS = 8
PEER_NKEYS = 128
PEER_TOPK = 16
PEER_QDIM = 256
PEER_HALF = PEER_QDIM // 2
PEER_EXPERTS = PEER_NKEYS * PEER_NKEYS
PEER_CHUNK = 128
IN_COLS = 3 * SC_WIDTH + 2 * CF_WIDTH + 2 * D_MODEL
EPS = 1e-6

kernel_name = 'hybrid_conv_memattn_peer_block'


def rms_norm(x, g):
    xf = x.astype(jnp.float32)
    y = xf * lax.rsqrt(jnp.mean(xf * xf, axis=-1, keepdims=True) + EPS)
    return (y * g.astype(jnp.float32)).astype(x.dtype)


def layer_norm(x, g, b):
    xf = x.astype(jnp.float32)
    mu = jnp.mean(xf, axis=-1, keepdims=True)
    var = jnp.mean(jnp.square(xf - mu), axis=-1, keepdims=True)
    y = (xf - mu) * lax.rsqrt(var + EPS)
    return (y * g.astype(jnp.float32) + b.astype(jnp.float32)).astype(x.dtype)


def causal_depthwise_conv(x, w):
    k, c = w.shape
    return lax.conv_general_dilated(
        x, w.reshape(k, 1, c).astype(x.dtype), window_strides=(1,), padding=[(k - 1, 0)],
        dimension_numbers=('NWC', 'WIO', 'NWC'), feature_group_count=c)


def conv_mixers(h, w_in, cf_b_pw1, sc_conv_w, sc_w_out, cf_conv_w, cf_conv_b, cf_ln_g, cf_ln_b,
                cf_w_pw2, cf_b_pw2, w_mix_out):
    proj = jnp.einsum('bsd,dc->bsc', h, w_in)
    cuts = [SC_WIDTH, 2 * SC_WIDTH, 3 * SC_WIDTH, 3 * SC_WIDTH + CF_WIDTH,
            3 * SC_WIDTH + 2 * CF_WIDTH, 3 * SC_WIDTH + 2 * CF_WIDTH + D_MODEL]
    sc_b, sc_c, sc_x, cf_a, cf_g, gate_sc, gate_cf = jnp.split(proj, cuts, axis=-1)
    y_sc = sc_b * causal_depthwise_conv(sc_c * sc_x, sc_conv_w)
    y_sc = jnp.einsum('bsc,cd->bsd', y_sc, sc_w_out)
    cf = (cf_a + cf_b_pw1[:CF_WIDTH]) * jax.nn.sigmoid(cf_g + cf_b_pw1[CF_WIDTH:])
    cf = causal_depthwise_conv(cf, cf_conv_w) + cf_conv_b
    cf = jax.nn.silu(layer_norm(cf, cf_ln_g, cf_ln_b))
    y_cf = jnp.einsum('bsc,cd->bsd', cf, cf_w_pw2) + cf_b_pw2
    merged = jax.nn.sigmoid(gate_sc) * y_sc + jax.nn.sigmoid(gate_cf) * y_cf
    return jnp.einsum('bsd,de->bse', merged, w_mix_out)


def memory_cross_attention(h, mem_n, w_q, w_kv, w_xo):
    b, s, _ = h.shape
    m = mem_n.shape[1]
    q = jnp.einsum('bsd,de->bse', h, w_q).reshape(b, s, XA_HEADS, XA_HEAD_DIM)
    kv = jnp.einsum('bmd,de->bme', mem_n, w_kv).reshape(b, m, 2, XA_HEADS, XA_HEAD_DIM)
    k, v = kv[:, :, 0], kv[:, :, 1]
    scores = jnp.einsum('bshd,bmhd->bhsm', q, k).astype(jnp.float32) * (XA_HEAD_DIM ** -0.5)
    probs = jax.nn.softmax(scores, axis=-1).astype(v.dtype)
    o = jnp.einsum('bhsm,bmhd->bshd', probs, v).reshape(b, s, XA_HEADS * XA_HEAD_DIM)
    return jnp.einsum('bse,ed->bsd', o, w_xo)


def peer_ffn(h, w_pq, subkeys, u, v):
    b, s, d = h.shape
    q = jnp.einsum('bsd,dq->bsq', h, w_pq).reshape(b, s, PEER_HEADS, 2, PEER_HALF)
    sub_scores = jnp.einsum('bshpc,pnc->bshpn', q, subkeys).astype(jnp.float32)
    top_s, top_i = lax.top_k(sub_scores, PEER_TOPK)
    n_cand = PEER_TOPK * PEER_TOPK
    cand_s = (top_s[..., 0, :, None] + top_s[..., 1, None, :]).reshape(b, s, PEER_HEADS, n_cand)
    cand_i = (top_i[..., 0, :, None] * PEER_NKEYS + top_i[..., 1, None, :]).reshape(b, s, PEER_HEADS, n_cand)
    fin_s, fin_pos = lax.top_k(cand_s, PEER_TOPK)
    expert_idx = jnp.take_along_axis(cand_i, fin_pos, axis=-1)
    gates = jax.nn.softmax(fin_s, axis=-1).astype(h.dtype)
    n_chunks = (b * s) // PEER_CHUNK
    h_c = h.reshape(n_chunks, PEER_CHUNK, d)
    e_c = expert_idx.reshape(n_chunks, PEER_CHUNK, PEER_HEADS, PEER_TOPK)
    g_c = gates.reshape(n_chunks, PEER_CHUNK, PEER_HEADS, PEER_TOPK)

    def chunk(args):
        hc, ec, gc = args
        u_sel = jnp.take(u, ec, axis=0)
        act = jax.nn.gelu(jnp.einsum('chkd,cd->chk', u_sel, hc), approximate=False) * gc
        v_sel = jnp.take(v, ec, axis=0)
        return jnp.einsum('chk,chkd->cd', act, v_sel)

    out = lax.map(chunk, (h_c, e_c, g_c))
    return out.reshape(b, s, d)


def setup_inputs(seed: int = 0) -> dict:
    key = jax.random.key(seed)
    ks = jax.random.split(key, 26)
    L = DEPTH

    def nrm(k, shape, scale):
        return jax.random.normal(k, shape, jnp.float32) * scale

    def gain(k, shape):
        return 1.0 + 0.02 * jax.random.normal(k, shape, jnp.float32)

    return {
        'x': nrm(ks[0], (BATCH, SEQ, D_MODEL), 1.0),
        'mem': nrm(ks[1], (BATCH, N_MEM, D_MODEL), 1.0),
        'norm_mix_g': gain(ks[2], (L, D_MODEL)),
        'w_in': nrm(ks[3], (L, D_MODEL, IN_COLS), D_MODEL ** -0.5),
        'cf_b_pw1': nrm(ks[4], (L, 2 * CF_WIDTH), 0.02),
        'sc_conv_w': nrm(ks[5], (L, SC_KERNEL, SC_WIDTH), SC_KERNEL ** -0.5),
        'sc_w_out': nrm(ks[6], (L, SC_WIDTH, D_MODEL), SC_WIDTH ** -0.5),
        'cf_conv_w': nrm(ks[7], (L, CF_KERNEL, CF_WIDTH), CF_KERNEL ** -0.5),
        'cf_conv_b': nrm(ks[8], (L, CF_WIDTH), 0.02),
        'cf_ln_g': gain(ks[9], (L, CF_WIDTH)),
        'cf_ln_b': nrm(ks[10], (L, CF_WIDTH), 0.02),
        'cf_w_pw2': nrm(ks[11], (L, CF_WIDTH, D_MODEL), CF_WIDTH ** -0.5),
        'cf_b_pw2': nrm(ks[12], (L, D_MODEL), 0.02),
        'w_mix_out': nrm(ks[13], (L, D_MODEL, D_MODEL), D_MODEL ** -0.5),
        'norm_xa_g': gain(ks[14], (L, D_MODEL)),
        'norm_mem_g': gain(ks[15], (L, D_MODEL)),
        'w_q': nrm(ks[16], (L, D_MODEL, XA_HEADS * XA_HEAD_DIM), D_MODEL ** -0.5),
        'w_kv': nrm(ks[17], (L, D_MODEL, 2 * XA_HEADS * XA_HEAD_DIM), D_MODEL ** -0.5),
        'w_xo': nrm(ks[18], (L, XA_HEADS * XA_HEAD_DIM, D_MODEL), D_MODEL ** -0.5),
        'norm_peer_g': gain(ks[19], (L, D_MODEL)),
        'w_peer_q': nrm(ks[20], (L, D_MODEL, PEER_HEADS * PEER_QDIM), D_MODEL ** -0.5),
        'peer_subkeys': nrm(ks[21], (L, 2, PEER_NKEYS, PEER_HALF), PEER_HALF ** -0.5),
        'peer_u': nrm(ks[22], (L, PEER_EXPERTS, D_MODEL), D_MODEL ** -0.5),
        'peer_v': nrm(ks[23], (L, PEER_EXPERTS, D_MODEL), PEER_HEADS ** -0.5),
        'final_norm_g': gain(ks[24], (D_MODEL,)),
    }


def reference(x, mem, norm_mix_g, w_in, cf_b_pw1, sc_conv_w, sc_w_out, cf_conv_w, cf_conv_b, cf_ln_g,
              cf_ln_b, cf_w_pw2, cf_b_pw2, w_mix_out, norm_xa_g, norm_mem_g, w_q, w_kv, w_xo, norm_peer_g,
              w_peer_q, peer_subkeys, peer_u, peer_v, final_norm_g):
    for l in range(DEPTH):
        x = x + conv_mixers(rms_norm(x, norm_mix_g[l]), w_in[l], cf_b_pw1[l], sc_conv_w[l], sc_w_out[l],
                            cf_conv_w[l], cf_conv_b[l], cf_ln_g[l], cf_ln_b[l], cf_w_pw2[l], cf_b_pw2[l],
                            w_mix_out[l])
        mem_n = rms_norm(mem, norm_mem_g[l])
        x = x + memory_cross_attention(rms_norm(x, norm_xa_g[l]), mem_n, w_q[l], w_kv[l], w_xo[l])
        x = x + peer_ffn(rms_norm(x, norm_peer_g[l]), w_peer_q[l], peer_subkeys[l], peer_u[l], peer_v[l])
    return rms_norm(x, final_norm_g)
```

```python
import functools

import jax
import jax.numpy as jnp
from jax import lax
from jax.experimental import pallas as pl
from jax.experimental.pallas import tpu as pltpu

_EPS = 1e-6
_LANES = 128
_SUBLANES = 8
_CF_HALO = 32
_SC_HALO = 8
_ROWS = 32
_VMEM_LIMIT = 56 * 1024 * 1024


def _row_loop(n_rows, chunk, body):
    def step(i, carry):
        body(pl.multiple_of(i * chunk, chunk))
        return carry
    lax.fori_loop(0, n_rows // chunk, step, 0)


def _rms_rows(x, g):
    ms = jnp.mean(x * x, axis=-1, keepdims=True)
    return x * lax.rsqrt(ms + _EPS) * g


def _const_spec(shape):
    nd = len(shape)
    return pl.BlockSpec(shape, lambda *_: (0,) * nd, pipeline_mode=pl.Buffered(1))


def _mix_kernel(x_ref, g_ref, win_ref, b1_ref, scw_ref, scwo_ref, cfw_ref, cfcb_ref, lng_ref, lnb_ref,
                pw2_ref, b2_ref, wmo_ref, o_ref, hb, t0, t1, cfbuf, scbuf, act, macc, *, sc_k, cf_k):
    T, D = x_ref.shape
    nlb = D // _LANES
    s = pl.program_id(1)

    @pl.when(s == 0)
    def _():
        cfbuf[:, 0:_CF_HALO, :] = jnp.zeros((nlb, _CF_HALO, _LANES), jnp.float32)
        scbuf[:, 0:_SC_HALO, :] = jnp.zeros((nlb, _SC_HALO, _LANES), jnp.float32)

    def norm_rows(r0):
        rows = pl.ds(r0, _ROWS)
        hb[rows, :] = _rms_rows(x_ref[rows, :], g_ref[...]).astype(jnp.bfloat16)
    _row_loop(T, _ROWS, norm_rows)

    def proj(col):
        return jnp.dot(hb[...], win_ref[:, col * D:(col + 1) * D], preferred_element_type=jnp.float32)

    t0[...] = proj(3)
    t1[...] = proj(4)

    def glu_rows(r0):
        rows = pl.ds(r0, _ROWS)
        a = t0[rows, :] + b1_ref[:, 0:D]
        g = t1[rows, :] + b1_ref[:, D:2 * D]
        v = a * jax.nn.sigmoid(g)
        for c in range(nlb):
            cfbuf[c, pl.ds(_CF_HALO + r0, _ROWS), :] = v[:, c * _LANES:(c + 1) * _LANES]
    _row_loop(T, _ROWS, glu_rows)

    def cf_rows(r0):
        rows = pl.ds(r0, _ROWS)
        accs = []
        for c in range(nlb):
            lanes = slice(c * _LANES, (c + 1) * _LANES)
            acc = jnp.broadcast_to(cfcb_ref[:, lanes], (_ROWS, _LANES))
            for k in range(cf_k):
                acc = acc + cfw_ref[k:k + 1, lanes] * cfbuf[c, pl.ds(r0 + _CF_HALO - (cf_k - 1) + k, _ROWS), :]
            accs.append(acc)
        tot = accs[0]
        for c in range(1, nlb):
            tot = tot + accs[c]
        mu = jnp.sum(tot, axis=-1, keepdims=True) * (1.0 / D)
        devs = [a - mu for a in accs]
        sq = devs[0] * devs[0]
        for c in range(1, nlb):
            sq = sq + devs[c] * devs[c]
        rstd = lax.rsqrt(jnp.sum(sq, axis=-1, keepdims=True) * (1.0 / D) + _EPS)
        for c in range(nlb):
            lanes = slice(c * _LANES, (c + 1) * _LANES)
            y = devs[c] * rstd * lng_ref[:, lanes] + lnb_ref[:, lanes]
            act[rows, lanes] = (y * jax.nn.sigmoid(y)).astype(jnp.bfloat16)
    _row_loop(T, _ROWS, cf_rows)

    t0[...] = jnp.dot(act[...], pw2_ref[...], preferred_element_type=jnp.float32)
    t1[...] = proj(6)

    def gate_cf_rows(r0):
        rows = pl.ds(r0, _ROWS)
        macc[rows, :] = jax.nn.sigmoid(t1[rows, :]) * (t0[rows, :] + b2_ref[...])
    _row_loop(T, _ROWS, gate_cf_rows)

    t0[...] = proj(1)
    t1[...] = proj(2)

    def cx_rows(r0):
        rows = pl.ds(r0, _ROWS)
        v = t0[rows, :] * t1[rows, :]
        for c in range(nlb):
            scbuf[c, pl.ds(_SC_HALO + r0, _ROWS), :] = v[:, c * _LANES:(c + 1) * _LANES]
    _row_loop(T, _ROWS, cx_rows)

    t0[...] = proj(0)

    def sc_rows(r0):
        rows = pl.ds(r0, _ROWS)
        for c in range(nlb):
            lanes = slice(c * _LANES, (c + 1) * _LANES)
            acc = scw_ref[0:1, lanes] * scbuf[c, pl.ds(r0 + _SC_HALO - (sc_k - 1), _ROWS), :]
            for k in range(1, sc_k):
                acc = acc + scw_ref[k:k + 1, lanes] * scbuf[c, pl.ds(r0 + _SC_HALO - (sc_k - 1) + k, _ROWS), :]
            act[rows, lanes] = (t0[rows, lanes] * acc).astype(jnp.bfloat16)
    _row_loop(T, _ROWS, sc_rows)

    t0[...] = jnp.dot(act[...], scwo_ref[...], preferred_element_type=jnp.float32)
    t1[...] = proj(5)

    def merge_rows(r0):
        rows = pl.ds(r0, _ROWS)
        m = macc[rows, :] + jax.nn.sigmoid(t1[rows, :]) * t0[rows, :]
        act[rows, :] = m.astype(jnp.bfloat16)
    _row_loop(T, _ROWS, merge_rows)

    o_ref[...] = x_ref[...] + jnp.dot(act[...], wmo_ref[...], preferred_element_type=jnp.float32)

    cfbuf[:, 0:_CF_HALO, :] = cfbuf[:, T:T + _CF_HALO, :]
    scbuf[:, 0:_SC_HALO, :] = scbuf[:, T:T + _SC_HALO, :]


def _conv_mixers(x, g, w_in, b1, sc_w, sc_wo, cf_w, cf_cb, ln_g, ln_b, pw2, b2, w_mo, *, tile):
    B, S, D = x.shape
    T = min(tile, S)
    assert S % T == 0 and T % _ROWS == 0 and D % _LANES == 0 and T >= _CF_HALO
    sc_k, cf_k = sc_w.shape[0], cf_w.shape[0]
    assert cf_k - 1 <= _CF_HALO and sc_k - 1 <= _SC_HALO
    nlb = D // _LANES
    bf = jnp.bfloat16
    row = lambda v: v.reshape(1, -1)
    kern = functools.partial(_mix_kernel, sc_k=sc_k, cf_k=cf_k)
    xspec = pl.BlockSpec((None, T, D), lambda b, s: (b, s, 0))
    args = (x, row(g), w_in.astype(bf), row(b1), sc_w, sc_wo.astype(bf), cf_w, row(cf_cb), row(ln_g), row(ln_b),
            pw2.astype(bf), row(b2), w_mo.astype(bf))
    return pl.pallas_call(
        kern,
        out_shape=jax.ShapeDtypeStruct((B, S, D), jnp.float32),
        grid=(B, S // T),
        in_specs=[xspec] + [_const_spec(a.shape) for a in args[1:]],
        out_specs=xspec,
        scratch_shapes=[
            pltpu.VMEM((T, D), bf),
            pltpu.VMEM((T, D), jnp.float32),
            pltpu.VMEM((T, D), jnp.float32),
            pltpu.VMEM((nlb, _CF_HALO + T, _LANES), jnp.float32),
            pltpu.VMEM((nlb, _SC_HALO + T, _LANES), jnp.float32),
            pltpu.VMEM((T, D), bf),
            pltpu.VMEM((T, D), jnp.float32),
        ],
        compiler_params=pltpu.CompilerParams(
            dimension_semantics=("arbitrary", "arbitrary"), vmem_limit_bytes=_VMEM_LIMIT),
        name="conv_mixers",
    )(*args)


def _memkv_kernel(mem_ref, g_ref, wkv_ref, k_ref, v_ref):
    E = k_ref.shape[-1]
    mn = _rms_rows(mem_ref[...], g_ref[...]).astype(jnp.bfloat16)
    kv = jnp.dot(mn, wkv_ref[...], preferred_element_type=jnp.float32)
    k_ref[...] = kv[:, 0:E].astype(jnp.bfloat16)
    v_ref[...] = kv[:, E:2 * E].astype(jnp.bfloat16)


def _mem_kv(mem, g, w_kv):
    B, M, D = mem.shape
    E = w_kv.shape[1] // 2
    bf = jnp.bfloat16
    out = jax.ShapeDtypeStruct((B, M, E), bf)
    ospec = pl.BlockSpec((None, M, E), lambda b: (b, 0, 0))
    return pl.pallas_call(
        _memkv_kernel,
        out_shape=(out, out),
        grid=(B,),
        in_specs=[pl.BlockSpec((None, M, D), lambda b: (b, 0, 0)), _const_spec((1, D)), _const_spec(w_kv.shape)],
        out_specs=(ospec, ospec),
        compiler_params=pltpu.CompilerParams(dimension_semantics=("arbitrary",), vmem_limit_bytes=_VMEM_LIMIT),
        name="mem_kv",
    )(mem, g.reshape(1, D), w_kv.astype(bf))


def _xattn_kernel(x_ref, g_ref, wq_ref, k_ref, v_ref, wo_ref, o_ref, hb, ob, *, heads):
    T, D = x_ref.shape
    E = wq_ref.shape[1]
    hd = E // heads

    def norm_rows(r0):
        rows = pl.ds(r0, _ROWS)
        hb[rows, :] = _rms_rows(x_ref[rows, :], g_ref[...]).astype(jnp.bfloat16)
    _row_loop(T, _ROWS, norm_rows)

    q = jnp.dot(hb[...], wq_ref[...], preferred_element_type=jnp.float32).astype(jnp.bfloat16)
    for h in range(heads):
        cols = slice(h * hd, (h + 1) * hd)
        sc = lax.dot_general(q[:, cols], k_ref[:, cols], (((1,), (1,)), ((), ())),
                             preferred_element_type=jnp.float32) * (hd ** -0.5)
        sc = sc - jnp.max(sc, axis=-1, keepdims=True)
        p = jnp.exp(sc)
        p = p / jnp.sum(p, axis=-1, keepdims=True)
        ob[:, cols] = jnp.dot(p.astype(jnp.bfloat16), v_ref[:, cols],
                              preferred_element_type=jnp.float32).astype(jnp.bfloat16)
    o_ref[...] = x_ref[...] + jnp.dot(ob[...], wo_ref[...], preferred_element_type=jnp.float32)


def _cross_attention(x, g, w_q, k, v, w_o, *, heads, tile):
    B, S, D = x.shape
    M, E = k.shape[1], k.shape[2]
    T = min(tile, S)
    assert S % T == 0 and T % _ROWS == 0
    bf = jnp.bfloat16
    xspec = pl.BlockSpec((None, T, D), lambda b, s: (b, s, 0))
    kvspec = pl.BlockSpec((None, M, E), lambda b, s: (b, 0, 0))
    return pl.pallas_call(
        functools.partial(_xattn_kernel, heads=heads),
        out_shape=jax.ShapeDtypeStruct((B, S, D), jnp.float32),
        grid=(B, S // T),
        in_specs=[xspec, _const_spec((1, D)), _const_spec(w_q.shape), kvspec, kvspec, _const_spec(w_o.shape)],
        out_specs=xspec,
        scratch_shapes=[pltpu.VMEM((T, D), bf), pltpu.VMEM((T, E), bf)],
        compiler_params=pltpu.CompilerParams(
            dimension_semantics=("arbitrary", "arbitrary"), vmem_limit_bytes=_VMEM_LIMIT),
        name="mem_cross_attention",
    )(x, g.reshape(1, D), w_q.astype(bf), k, v, w_o.astype(bf))


def _batcher_pairs(n):
    pairs = []
    p = 1
    while p < n:
        k = p
        while k >= 1:
            for j in range(k % p, n - k, 2 * k):
                for i in range(min(k, n - j - k)):
                    if (i + j) // (2 * p) == (i + j + k) // (2 * p):
                        pairs.append((i + j, i + j + k))
            k //= 2
        p *= 2
    return pairs


def _sort_desc(xs):
    xs = list(xs)
    for i, j in _batcher_pairs(len(xs)):
        hi, lo = jnp.maximum(xs[i], xs[j]), jnp.minimum(xs[i], xs[j])
        xs[i], xs[j] = hi, lo
    return xs


def _bitonic_sort_desc(xs):
    xs = list(xs)
    n = len(xs)
    stride = n // 2
    while stride >= 1:
        for i in range(n):
            if i & stride == 0:
                hi, lo = jnp.maximum(xs[i], xs[i + stride]), jnp.minimum(xs[i], xs[i + stride])
                xs[i], xs[i + stride] = hi, lo
        stride //= 2
    return xs


def _sublane_partner(xs, shift):
    return [pltpu.roll(x, shift, 0) for x in xs]


def _top_sorted(score, k):
    n = score.shape[0]
    assert n == k * _SUBLANES
    xs = _sort_desc([score[g * _SUBLANES:(g + 1) * _SUBLANES, :] for g in range(k)])
    for shift in (4, 2, 1):
        ps = _sublane_partner(xs, shift)
        xs = _bitonic_sort_desc([jnp.maximum(xs[r], ps[k - 1 - r]) for r in range(k)])
    return xs


def _pair_threshold(a, b, k):
    assert k == 16
    neg = jnp.float32(-jnp.inf)
    sub = lax.broadcasted_iota(jnp.int32, a[0].shape, 0)

    def stack(vals):
        out = vals[-1]
        for s in range(len(vals) - 2, -1, -1):
            out = jnp.where(sub == s, vals[s], out)
        return out

    a_lo, a_hi = stack(a[0:8]), stack(a[8:16])
    b_lo, b_hi = stack(b[0:8]), stack(b[8:16])
    mid = sub >= 2
    cands = [
        a[0] + b_lo, a[0] + b_hi, a[1] + b_lo, a_hi + b[0],
        jnp.where(mid, a_lo + b[0], neg), jnp.where(mid, a_lo + b[1], neg),
        jnp.where(sub < 6,
                  stack([a[2], a[2], a[2], a[3], a[3], a[4], a[4], a[4]])
                  + stack([b[2], b[3], b[4], b[2], b[3], b[2], b[2], b[2]]), neg),
        jnp.full(a[0].shape, neg, jnp.float32),
    ]
    xs = _sort_desc(cands)
    ps = _sublane_partner(xs, 4)
    hi = _bitonic_sort_desc([jnp.maximum(xs[r], ps[7 - r]) for r in range(8)])
    lo = _bitonic_sort_desc([jnp.minimum(xs[r], ps[7 - r]) for r in range(8)])
    xs = hi + lo
    ps = _sublane_partner(xs, 2)
    xs = _bitonic_sort_desc([jnp.maximum(xs[r], ps[k - 1 - r]) for r in range(k)])
    ps = _sublane_partner(xs, 1)
    tau = jnp.maximum(xs[0], ps[k - 1])
    for r in range(1, k):
        tau = jnp.minimum(tau, jnp.maximum(xs[r], ps[k - 1 - r]))
    top = a[0] + b[0]
    z = jnp.zeros_like(tau)
    for c in cands[:-1]:
        z = z + jnp.where(c >= tau, jnp.exp(c - top), 0.0)
    z = jnp.sum(z, axis=0, keepdims=True)
    return tau, jnp.broadcast_to(z, tau.shape)


def _peer_kernel(x_ref, g_ref, wq_ref, sk_ref, u_ref, vt_ref, fg_ref, o_ref,
                 ht, s0s, s1s, e1s, cps, taus, acc, at, *, heads, topk):
    T, D = x_ref.shape
    et = u_ref.shape[0]
    nk = sk_ref.shape[1]
    half = sk_ref.shape[2]
    j = pl.program_id(1)

    @pl.when(j == 0)
    def _route():
        h = _rms_rows(x_ref[...], g_ref[...])
        ht[...] = h.T.astype(jnp.bfloat16)
        acc[...] = jnp.zeros_like(acc)
        for hd in range(heads):
            q = jnp.dot(wq_ref[hd * 2 * half:(hd + 1) * 2 * half, :], ht[...],
                        preferred_element_type=jnp.float32).astype(jnp.bfloat16)
            s0 = jnp.dot(sk_ref[0], q[0:half, :], preferred_element_type=jnp.float32)
            s1 = jnp.dot(sk_ref[1], q[half:2 * half, :], preferred_element_type=jnp.float32)
            a = _top_sorted(s0, topk)
            b = _top_sorted(s1, topk)
            tau, z = _pair_threshold(a, b, topk)
            s0s[hd] = s0
            s1s[hd] = s1
            e1s[hd] = jnp.exp(s1 - b[0][0:1, :])
            cps[hd] = jnp.exp(s0 - a[0][0:1, :]) / z[0:1, :]
            taus[hd] = tau

    st = jnp.dot(u_ref[...], ht[...], preferred_element_type=jnp.float32)
    for ii in range(et // nk):
        i = j * (et // nk) + ii
        rows = slice(ii * nk, (ii + 1) * nk)
        w = jnp.zeros((nk, T), jnp.float32)
        for hd in range(heads):
            mask = (s0s[hd, pl.ds(i, 1), :] + s1s[hd]) >= taus[hd, 0:1, :]
            w = w + jnp.where(mask, e1s[hd], 0.0) * cps[hd, pl.ds(i, 1), :]
        s = st[rows, :]
        gelu = 0.5 * s * (1.0 + lax.erf(s * (2.0 ** -0.5)))
        at[rows, :] = (gelu * w).astype(jnp.bfloat16)
    acc[...] += jnp.dot(vt_ref[...], at[...], preferred_element_type=jnp.float32)

    @pl.when(j == pl.num_programs(1) - 1)
    def _finish():
        y = x_ref[...] + acc[...].T
        o_ref[...] = _rms_rows(y, fg_ref[...])


def _peer_final(x, g, w_pq, subkeys, u, v, final_g, *, heads, topk, tile, etile):
    N, D = x.shape
    ne = u.shape[0]
    nk, half = subkeys.shape[1], subkeys.shape[2]
    T = min(tile, N)
    assert N % T == 0 and T % _LANES == 0 and ne == nk * nk and ne % etile == 0 and etile % nk == 0
    assert nk == topk * _SUBLANES and w_pq.shape[1] == heads * 2 * half
    bf = jnp.bfloat16
    f32 = jnp.float32
    n_et = ne // etile
    vt = v.astype(bf).reshape(n_et, etile, D).transpose(0, 2, 1)
    xspec = pl.BlockSpec((T, D), lambda i, j: (i, 0))
    return pl.pallas_call(
        functools.partial(_peer_kernel, heads=heads, topk=topk),
        out_shape=jax.ShapeDtypeStruct((N, D), f32),
        grid=(N // T, n_et),
        in_specs=[xspec, _const_spec((1, D)), _const_spec((w_pq.shape[1], D)), _const_spec(subkeys.shape),
                  pl.BlockSpec((etile, D), lambda i, j: (j, 0)),
                  pl.BlockSpec((None, D, etile), lambda i, j: (j, 0, 0)),
                  _const_spec((1, D))],
        out_specs=xspec,
        scratch_shapes=[
            pltpu.VMEM((D, T), bf),
            pltpu.VMEM((heads, nk, T), f32),
            pltpu.VMEM((heads, nk, T), f32),
            pltpu.VMEM((heads, nk, T), f32),
            pltpu.VMEM((heads, nk, T), f32),
            pltpu.VMEM((heads, _SUBLANES, T), f32),
            pltpu.VMEM((D, T), f32),
            pltpu.VMEM((etile, T), bf),
        ],
        compiler_params=pltpu.CompilerParams(
            dimension_semantics=("arbitrary", "arbitrary"), vmem_limit_bytes=_VMEM_LIMIT),
        name="peer_final",
    )(x, g.reshape(1, D), w_pq.T.astype(bf), subkeys.astype(bf), u.astype(bf), vt, final_g.reshape(1, D))


def kernel(x, mem, norm_mix_g, w_in, cf_b_pw1, sc_conv_w, sc_w_out, cf_conv_w, cf_conv_b, cf_ln_g, cf_ln_b, cf_w_pw2, cf_b_pw2, w_mix_out, norm_xa_g, norm_mem_g, w_q, w_kv, w_xo, norm_peer_g, w_peer_q, peer_subkeys, peer_u, peer_v, final_norm_g):
    B, S, D = x.shape
    depth = norm_mix_g.shape[0]
    heads_xa = 4
    heads_peer = 8
    topk = 16
    for l in range(depth):
        x = _conv_mixers(x, norm_mix_g[l], w_in[l], cf_b_pw1[l], sc_conv_w[l], sc_w_out[l], cf_conv_w[l], cf_conv_b[l],
                         cf_ln_g[l], cf_ln_b[l], cf_w_pw2[l], cf_b_pw2[l], w_mix_out[l], tile=512)
        k, v = _mem_kv(mem, norm_mem_g[l], w_kv[l])
        x = _cross_attention(x, norm_xa_g[l], w_q[l], k, v, w_xo[l], heads=heads_xa, tile=512)
        last = l == depth - 1
        fg = final_norm_g if last else None
        assert last, "final RMSNorm is fused into the last PEER call"
        x = _peer_final(x.reshape(B * S, D), norm_peer_g[l], w_peer_q[l], peer_subkeys[l], peer_u[l], peer_v[l], fg,
                        heads=heads_peer, topk=topk, tile=512, etile=512).reshape(B, S, D)
    return x
```

```python
import functools

import jax
import jax.numpy as jnp
from jax import lax
from jax.experimental import pallas as pl
from jax.experimental.pallas import tpu as pltpu

_EPS = 1e-6
_LANES = 128
_SUBLANES = 8
_CF_HALO = 32
_SC_HALO = 8
_ROWS = 32
_VMEM_LIMIT = 56 * 1024 * 1024


def _row_loop(n_rows, chunk, body):
    def step(i, carry):
        body(pl.multiple_of(i * chunk, chunk))
        return carry
    lax.fori_loop(0, n_rows // chunk, step, 0)


def _rms_rows(x, g):
    ms = jnp.mean(x * x, axis=-1, keepdims=True)
    return x * lax.rsqrt(ms + _EPS) * g


def _const_spec(shape):
    nd = len(shape)
    return pl.BlockSpec(shape, lambda *_: (0,) * nd, pipeline_mode=pl.Buffered(1))


def _mix_kernel(x_ref, g_ref, win_ref, b1_ref, scw_ref, scwo_ref, cfw_ref, cfcb_ref, lng_ref, lnb_ref,
                pw2_ref, b2_ref, wmo_ref, o_ref, hb, t0, t1, cfbuf, scbuf, act, macc, *, sc_k, cf_k):
    T, D = x_ref.shape
    nlb = D // _LANES
    s = pl.program_id(1)

    @pl.when(s == 0)
    def _():
        cfbuf[:, 0:_CF_HALO, :] = jnp.zeros((nlb, _CF_HALO, _LANES), jnp.float32)
        scbuf[:, 0:_SC_HALO, :] = jnp.zeros((nlb, _SC_HALO, _LANES), jnp.float32)

    def norm_rows(r0):
        rows = pl.ds(r0, _ROWS)
        hb[rows, :] = _rms_rows(x_ref[rows, :], g_ref[...]).astype(jnp.bfloat16)
    _row_loop(T, _ROWS, norm_rows)

    def proj(col):
        return jnp.dot(hb[...], win_ref[:, col * D:(col + 1) * D], preferred_element_type=jnp.float32)

    t0[...] = proj(3)
    t1[...] = proj(4)

    def glu_rows(r0):
        rows = pl.ds(r0, _ROWS)
        a = t0[rows, :] + b1_ref[:, 0:D]
        g = t1[rows, :] + b1_ref[:, D:2 * D]
        v = a * jax.nn.sigmoid(g)
        for c in range(nlb):
            cfbuf[c, pl.ds(_CF_HALO + r0, _ROWS), :] = v[:, c * _LANES:(c + 1) * _LANES]
    _row_loop(T, _ROWS, glu_rows)

    def cf_rows(r0):
        rows = pl.ds(r0, _ROWS)
        accs = []
        for c in range(nlb):
            lanes = slice(c * _LANES, (c + 1) * _LANES)
            acc = jnp.broadcast_to(cfcb_ref[:, lanes], (_ROWS, _LANES))
            for k in range(cf_k):
                acc = acc + cfw_ref[k:k + 1, lanes] * cfbuf[c, pl.ds(r0 + _CF_HALO - (cf_k - 1) + k, _ROWS), :]
            accs.append(acc)
        tot = accs[0]
        for c in range(1, nlb):
            tot = tot + accs[c]
        mu = jnp.sum(tot, axis=-1, keepdims=True) * (1.0 / D)
        devs = [a - mu for a in accs]
        sq = devs[0] * devs[0]
        for c in range(1, nlb):
            sq = sq + devs[c] * devs[c]
        rstd = lax.rsqrt(jnp.sum(sq, axis=-1, keepdims=True) * (1.0 / D) + _EPS)
        for c in range(nlb):
            lanes = slice(c * _LANES, (c + 1) * _LANES)
            y = devs[c] * rstd * lng_ref[:, lanes] + lnb_ref[:, lanes]
            act[rows, lanes] = (y * jax.nn.sigmoid(y)).astype(jnp.bfloat16)
    _row_loop(T, _ROWS, cf_rows)

    t0[...] = jnp.dot(act[...], pw2_ref[...], preferred_element_type=jnp.float32)
    t1[...] = proj(6)

    def gate_cf_rows(r0):
        rows = pl.ds(r0, _ROWS)
        macc[rows, :] = jax.nn.sigmoid(t1[rows, :]) * (t0[rows, :] + b2_ref[...])
    _row_loop(T, _ROWS, gate_cf_rows)

    t0[...] = proj(1)
    t1[...] = proj(2)

    def cx_rows(r0):
        rows = pl.ds(r0, _ROWS)
        v = t0[rows, :] * t1[rows, :]
        for c in range(nlb):
            scbuf[c, pl.ds(_SC_HALO + r0, _ROWS), :] = v[:, c * _LANES:(c + 1) * _LANES]
    _row_loop(T, _ROWS, cx_rows)

    t0[...] = proj(0)

    def sc_rows(r0):
        rows = pl.ds(r0, _ROWS)
        for c in range(nlb):
            lanes = slice(c * _LANES, (c + 1) * _LANES)
            acc = scw_ref[0:1, lanes] * scbuf[c, pl.ds(r0 + _SC_HALO - (sc_k - 1), _ROWS), :]
            for k in range(1, sc_k):
                acc = acc + scw_ref[k:k + 1, lanes] * scbuf[c, pl.ds(r0 + _SC_HALO - (sc_k - 1) + k, _ROWS), :]
            act[rows, lanes] = (t0[rows, lanes] * acc).astype(jnp.bfloat16)
    _row_loop(T, _ROWS, sc_rows)

    t0[...] = jnp.dot(act[...], scwo_ref[...], preferred_element_type=jnp.float32)
    t1[...] = proj(5)

    def merge_rows(r0):
        rows = pl.ds(r0, _ROWS)
        m = macc[rows, :] + jax.nn.sigmoid(t1[rows, :]) * t0[rows, :]
        act[rows, :] = m.astype(jnp.bfloat16)
    _row_loop(T, _ROWS, merge_rows)

    o_ref[...] = x_ref[...] + jnp.dot(act[...], wmo_ref[...], preferred_element_type=jnp.float32)

    cfbuf[:, 0:_CF_HALO, :] = cfbuf[:, T:T + _CF_HALO, :]
    scbuf[:, 0:_SC_HALO, :] = scbuf[:, T:T + _SC_HALO, :]


def _conv_mixers(x, g, w_in, b1, sc_w, sc_wo, cf_w, cf_cb, ln_g, ln_b, pw2, b2, w_mo, *, tile):
    B, S, D = x.shape
    T = min(tile, S)
    assert S % T == 0 and T % _ROWS == 0 and D % _LANES == 0 and T >= _CF_HALO
    sc_k, cf_k = sc_w.shape[0], cf_w.shape[0]
    assert cf_k - 1 <= _CF_HALO and sc_k - 1 <= _SC_HALO
    nlb = D // _LANES
    bf = jnp.bfloat16
    row = lambda v: v.reshape(1, -1)
    kern = functools.partial(_mix_kernel, sc_k=sc_k, cf_k=cf_k)
    xspec = pl.BlockSpec((None, T, D), lambda b, s: (b, s, 0))
    args = (x, row(g), w_in.astype(bf), row(b1), sc_w, sc_wo.astype(bf), cf_w, row(cf_cb), row(ln_g), row(ln_b),
            pw2.astype(bf), row(b2), w_mo.astype(bf))
    return pl.pallas_call(
        kern,
        out_shape=jax.ShapeDtypeStruct((B, S, D), jnp.float32),
        grid=(B, S // T),
        in_specs=[xspec] + [_const_spec(a.shape) for a in args[1:]],
        out_specs=xspec,
        scratch_shapes=[
            pltpu.VMEM((T, D), bf),
            pltpu.VMEM((T, D), jnp.float32),
            pltpu.VMEM((T, D), jnp.float32),
            pltpu.VMEM((nlb, _CF_HALO + T, _LANES), jnp.float32),
            pltpu.VMEM((nlb, _SC_HALO + T, _LANES), jnp.float32),
            pltpu.VMEM((T, D), bf),
            pltpu.VMEM((T, D), jnp.float32),
        ],
        compiler_params=pltpu.CompilerParams(
            dimension_semantics=("arbitrary", "arbitrary"), vmem_limit_bytes=_VMEM_LIMIT),
        name="conv_mixers",
    )(*args)


def _memkv_kernel(mem_ref, g_ref, wkv_ref, k_ref, v_ref):
    E = k_ref.shape[-1]
    mn = _rms_rows(mem_ref[...], g_ref[...]).astype(jnp.bfloat16)
    kv = jnp.dot(mn, wkv_ref[...], preferred_element_type=jnp.float32)
    k_ref[...] = kv[:, 0:E].astype(jnp.bfloat16)
    v_ref[...] = kv[:, E:2 * E].astype(jnp.bfloat16)


def _mem_kv(mem, g, w_kv):
    B, M, D = mem.shape
    E = w_kv.shape[1] // 2
    bf = jnp.bfloat16
    out = jax.ShapeDtypeStruct((B, M, E), bf)
    ospec = pl.BlockSpec((None, M, E), lambda b: (b, 0, 0))
    return pl.pallas_call(
        _memkv_kernel,
        out_shape=(out, out),
        grid=(B,),
        in_specs=[pl.BlockSpec((None, M, D), lambda b: (b, 0, 0)), _const_spec((1, D)), _const_spec(w_kv.shape)],
        out_specs=(ospec, ospec),
        compiler_params=pltpu.CompilerParams(dimension_semantics=("arbitrary",), vmem_limit_bytes=_VMEM_LIMIT),
        name="mem_kv",
    )(mem, g.reshape(1, D), w_kv.astype(bf))


def _xattn_kernel(x_ref, g_ref, wq_ref, k_ref, v_ref, wo_ref, o_ref, hb, ob, *, heads):
    T, D = x_ref.shape
    E = wq_ref.shape[1]
    hd = E // heads

    def norm_rows(r0):
        rows = pl.ds(r0, _ROWS)
        hb[rows, :] = _rms_rows(x_ref[rows, :], g_ref[...]).astype(jnp.bfloat16)
    _row_loop(T, _ROWS, norm_rows)

    q = jnp.dot(hb[...], wq_ref[...], preferred_element_type=jnp.float32).astype(jnp.bfloat16)
    for h in range(heads):
        cols = slice(h * hd, (h + 1) * hd)
        sc = lax.dot_general(q[:, cols], k_ref[:, cols], (((1,), (1,)), ((), ())),
                             preferred_element_type=jnp.float32) * (hd ** -0.5)
        sc = sc - jnp.max(sc, axis=-1, keepdims=True)
        p = jnp.exp(sc)
        p = p / jnp.sum(p, axis=-1, keepdims=True)
        ob[:, cols] = jnp.dot(p.astype(jnp.bfloat16), v_ref[:, cols],
                              preferred_element_type=jnp.float32).astype(jnp.bfloat16)
    o_ref[...] = x_ref[...] + jnp.dot(ob[...], wo_ref[...], preferred_element_type=jnp.float32)


def _cross_attention(x, g, w_q, k, v, w_o, *, heads, tile):
    B, S, D = x.shape
    M, E = k.shape[1], k.shape[2]
    T = min(tile, S)
    assert S % T == 0 and T % _ROWS == 0
    bf = jnp.bfloat16
    xspec = pl.BlockSpec((None, T, D), lambda b, s: (b, s, 0))
    kvspec = pl.BlockSpec((None, M, E), lambda b, s: (b, 0, 0))
    return pl.pallas_call(
        functools.partial(_xattn_kernel, heads=heads),
        out_shape=jax.ShapeDtypeStruct((B, S, D), jnp.float32),
        grid=(B, S // T),
        in_specs=[xspec, _const_spec((1, D)), _const_spec(w_q.shape), kvspec, kvspec, _const_spec(w_o.shape)],
        out_specs=xspec,
        scratch_shapes=[pltpu.VMEM((T, D), bf), pltpu.VMEM((T, E), bf)],
        compiler_params=pltpu.CompilerParams(
            dimension_semantics=("arbitrary", "arbitrary"), vmem_limit_bytes=_VMEM_LIMIT),
        name="mem_cross_attention",
    )(x, g.reshape(1, D), w_q.astype(bf), k, v, w_o.astype(bf))


def _batcher_pairs(n):
    pairs = []
    p = 1
    while p < n:
        k = p
        while k >= 1:
            for j in range(k % p, n - k, 2 * k):
                for i in range(min(k, n - j - k)):
                    if (i + j) // (2 * p) == (i + j + k) // (2 * p):
                        pairs.append((i + j, i + j + k))
            k //= 2
        p *= 2
    return pairs


def _sort_desc(xs):
    xs = list(xs)
    for i, j in _batcher_pairs(len(xs)):
        hi, lo = jnp.maximum(xs[i], xs[j]), jnp.minimum(xs[i], xs[j])
        xs[i], xs[j] = hi, lo
    return xs


def _bitonic_sort_desc(xs):
    xs = list(xs)
    n = len(xs)
    stride = n // 2
    while stride >= 1:
        for i in range(n):
            if i & stride == 0:
                hi, lo = jnp.maximum(xs[i], xs[i + stride]), jnp.minimum(xs[i], xs[i + stride])
                xs[i], xs[i + stride] = hi, lo
        stride //= 2
    return xs


def _sublane_partner(xs, shift):
    return [pltpu.roll(x, shift, 0) for x in xs]


def _top_sorted(score, k):
    n = score.shape[0]
    assert n == k * _SUBLANES
    xs = _sort_desc([score[g * _SUBLANES:(g + 1) * _SUBLANES, :] for g in range(k)])
    for shift in (4, 2, 1):
        ps = _sublane_partner(xs, shift)
        xs = _bitonic_sort_desc([jnp.maximum(xs[r], ps[k - 1 - r]) for r in range(k)])
    return xs


def _pair_threshold(a, b, k):
    assert k == 16
    neg = jnp.float32(-jnp.inf)
    sub = lax.broadcasted_iota(jnp.int32, a[0].shape, 0)

    def stack(vals):
        out = vals[-1]
        for s in range(len(vals) - 2, -1, -1):
            out = jnp.where(sub == s, vals[s], out)
        return out

    a_lo, a_hi = stack(a[0:8]), stack(a[8:16])
    b_lo, b_hi = stack(b[0:8]), stack(b[8:16])
    mid = sub >= 2
    cands = [
        a[0] + b_lo, a[0] + b_hi, a[1] + b_lo, a_hi + b[0],
        jnp.where(mid, a_lo + b[0], neg), jnp.where(mid, a_lo + b[1], neg),
        jnp.where(sub < 6,
                  stack([a[2], a[2], a[2], a[3], a[3], a[4], a[4], a[4]])
                  + stack([b[2], b[3], b[4], b[2], b[3], b[2], b[2], b[2]]), neg),
        jnp.full(a[0].shape, neg, jnp.float32),
    ]
    xs = _sort_desc(cands)
    ps = _sublane_partner(xs, 4)
    hi = _bitonic_sort_desc([jnp.maximum(xs[r], ps[7 - r]) for r in range(8)])
    lo = _bitonic_sort_desc([jnp.minimum(xs[r], ps[7 - r]) for r in range(8)])
    xs = hi + lo
    ps = _sublane_partner(xs, 2)
    xs = _bitonic_sort_desc([jnp.maximum(xs[r], ps[k - 1 - r]) for r in range(k)])
    ps = _sublane_partner(xs, 1)
    tau = jnp.maximum(xs[0], ps[k - 1])
    for r in range(1, k):
        tau = jnp.minimum(tau, jnp.maximum(xs[r], ps[k - 1 - r]))
    top = a[0] + b[0]
    z = jnp.zeros_like(tau)
    for c in cands[:-1]:
        z = z + jnp.where(c >= tau, jnp.exp(c - top), 0.0)
    z = jnp.sum(z, axis=0, keepdims=True)
    return tau, jnp.broadcast_to(z, tau.shape)


_PK = 16


def _pack_rows(lo, hi):
    return jnp.concatenate([lo, hi], axis=0).astype(jnp.bfloat16)


def _rank_among(s, b):
    r = jnp.full(s.shape, float(len(b)), jnp.float32)
    for k in range(len(b) - 1, -1, -1):
        r = jnp.where(s >= b[k], float(k), r)
    return r


def _pair_count(s, b, tau):
    c = jnp.zeros(s.shape, jnp.float32)
    for k in range(len(b)):
        c = jnp.where(s + b[k] >= tau, float(k + 1), c)
    return c


def _peer_kernel(x_ref, g_ref, wq_ref, sk_ref, u_ref, vt_ref, fg_ref, o_ref,
                 ht, cnts, cps, r1s, e1s, acc, st, wbuf, at, *, heads, topk, ehalf):
    T, D = x_ref.shape
    et = u_ref.shape[0]
    nk = sk_ref.shape[1]
    half = sk_ref.shape[2]
    j = pl.program_id(1)
    bf = jnp.bfloat16

    @pl.when(j == 0)
    def _route():
        h = _rms_rows(x_ref[...], g_ref[...])
        ht[...] = h.T.astype(bf)
        acc[...] = jnp.zeros_like(acc)
        for hd in range(heads):
            q = jnp.dot(wq_ref[hd * 2 * half:(hd + 1) * 2 * half, :], ht[...],
                        preferred_element_type=jnp.float32).astype(bf)
            s0 = jnp.dot(sk_ref[0], q[0:half, :], preferred_element_type=jnp.float32)
            s1 = jnp.dot(sk_ref[1], q[half:2 * half, :], preferred_element_type=jnp.float32)
            a = _top_sorted(s0, topk)
            b = _top_sorted(s1, topk)
            tau, z = _pair_threshold(a, b, topk)
            for gq in range(nk // _SUBLANES):
                rows = slice(gq * _SUBLANES, (gq + 1) * _SUBLANES)
                cnts[hd, rows, :] = _pair_count(s0[rows, :], b, tau)
                cps[hd, rows, :] = jnp.exp(s0[rows, :] - a[0]) * ((2.0 ** -0.5) / z)
            for gq in range(nk // _PK):
                rows = slice(gq * _PK, (gq + 1) * _PK)
                lo = s1[gq * _PK:gq * _PK + _SUBLANES, :]
                hi = s1[gq * _PK + _SUBLANES:(gq + 1) * _PK, :]
                r1s[hd, rows, :] = _pack_rows(_rank_among(lo, b), _rank_among(hi, b))
                e1s[hd, rows, :] = _pack_rows(jnp.exp(lo - b[0]), jnp.exp(hi - b[0]))

    n_sub = et // ehalf
    per = ehalf // nk
    for sb in range(n_sub):
        for ii in range(per):
            i = (j * n_sub + sb) * per + ii
            w = [jnp.zeros((_PK, T), bf) for _ in range(nk // _PK)]
            for hd in range(heads):
                cnt = jnp.broadcast_to(cnts[hd, pl.ds(i, 1), :], (_SUBLANES, T))
                cp = jnp.broadcast_to(cps[hd, pl.ds(i, 1), :], (_SUBLANES, T))
                cnt = _pack_rows(cnt, cnt)
                cp = _pack_rows(cp, cp)
                for gq in range(nk // _PK):
                    rows = slice(gq * _PK, (gq + 1) * _PK)
                    e1 = e1s[hd, rows, :]
                    w[gq] = w[gq] + jnp.where(r1s[hd, rows, :] < cnt, e1, jnp.zeros_like(e1)) * cp
            for gq in range(nk // _PK):
                wbuf[sb, ii * nk + gq * _PK:ii * nk + (gq + 1) * _PK, :] = w[gq]
    for sb in range(n_sub):
        st[sb] = jnp.dot(u_ref[sb * ehalf:(sb + 1) * ehalf, :], ht[...], preferred_element_type=jnp.float32)
    for sb in range(n_sub):
        for gq in range(ehalf // _PK):
            rows = slice(gq * _PK, (gq + 1) * _PK)
            z = st[sb, rows, :] * (2.0 ** -0.5)
            g = z * (1.0 + lax.erf(z))
            at[sb, rows, :] = _pack_rows(g[0:_SUBLANES, :], g[_SUBLANES:_PK, :]) * wbuf[sb, rows, :]
    out = jnp.dot(vt_ref[:, 0:ehalf], at[0], preferred_element_type=jnp.float32)
    for sb in range(1, n_sub):
        out = out + jnp.dot(vt_ref[:, sb * ehalf:(sb + 1) * ehalf], at[sb], preferred_element_type=jnp.float32)
    acc[...] += out

    @pl.when(j == pl.num_programs(1) - 1)
    def _finish():
        y = x_ref[...] + acc[...].T
        o_ref[...] = _rms_rows(y, fg_ref[...])


def _peer_final(x, g, w_pq, subkeys, u, v, final_g, *, heads, topk, tile, etile):
    N, D = x.shape
    ne = u.shape[0]
    nk, half = subkeys.shape[1], subkeys.shape[2]
    T = min(tile, N)
    assert N % T == 0 and T % _LANES == 0 and ne == nk * nk and ne % etile == 0 and etile % nk == 0
    assert nk == topk * _SUBLANES and w_pq.shape[1] == heads * 2 * half
    ehalf = min(etile, 2 * nk)
    assert ehalf % nk == 0 and etile % ehalf == 0
    bf = jnp.bfloat16
    f32 = jnp.float32
    n_et = ne // etile
    vt = v.astype(bf).reshape(n_et, etile, D).transpose(0, 2, 1)
    xspec = pl.BlockSpec((T, D), lambda i, j: (i, 0))
    return pl.pallas_call(
        functools.partial(_peer_kernel, heads=heads, topk=topk, ehalf=ehalf),
        out_shape=jax.ShapeDtypeStruct((N, D), f32),
        grid=(N // T, n_et),
        in_specs=[xspec, _const_spec((1, D)), _const_spec((w_pq.shape[1], D)), _const_spec(subkeys.shape),
                  pl.BlockSpec((etile, D), lambda i, j: (j, 0)),
                  pl.BlockSpec((None, D, etile), lambda i, j: (j, 0, 0)),
                  _const_spec((1, D))],
        out_specs=xspec,
        scratch_shapes=[
            pltpu.VMEM((D, T), bf),
            pltpu.VMEM((heads, nk, T), f32),
            pltpu.VMEM((heads, nk, T), f32),
            pltpu.VMEM((heads, nk, T), bf),
            pltpu.VMEM((heads, nk, T), bf),
            pltpu.VMEM((D, T), f32),
            pltpu.VMEM((etile // ehalf, ehalf, T), f32),
            pltpu.VMEM((etile // ehalf, ehalf, T), bf),
            pltpu.VMEM((etile // ehalf, ehalf, T), bf),
        ],
        compiler_params=pltpu.CompilerParams(
            dimension_semantics=("arbitrary", "arbitrary"), vmem_limit_bytes=_VMEM_LIMIT),
        name="peer_final",
    )(x, g.reshape(1, D), w_pq.T.astype(bf), subkeys.astype(bf), u.astype(bf), vt, final_g.reshape(1, D))


def kernel(x, mem, norm_mix_g, w_in, cf_b_pw1, sc_conv_w, sc_w_out, cf_conv_w, cf_conv_b, cf_ln_g, cf_ln_b, cf_w_pw2, cf_b_pw2, w_mix_out, norm_xa_g, norm_mem_g, w_q, w_kv, w_xo, norm_peer_g, w_peer_q, peer_subkeys, peer_u, peer_v, final_norm_g):
    B, S, D = x.shape
    depth = norm_mix_g.shape[0]
    heads_xa = 4
    heads_peer = 8
    topk = 16
    for l in range(depth):
        x = _conv_mixers(x, norm_mix_g[l], w_in[l], cf_b_pw1[l], sc_conv_w[l], sc_w_out[l], cf_conv_w[l], cf_conv_b[l],
                         cf_ln_g[l], cf_ln_b[l], cf_w_pw2[l], cf_b_pw2[l], w_mix_out[l], tile=512)
        k, v = _mem_kv(mem, norm_mem_g[l], w_kv[l])
        x = _cross_attention(x, norm_xa_g[l], w_q[l], k, v, w_xo[l], heads=heads_xa, tile=512)
        last = l == depth - 1
        fg = final_norm_g if last else None
        assert last, "final RMSNorm is fused into the last PEER call"
        x = _peer_final(x.reshape(B * S, D), norm_peer_g[l], w_peer_q[l], peer_subkeys[l], peer_u[l], peer_v[l], fg,
                        heads=heads_peer, topk=topk, tile=512, etile=1024).reshape(B, S, D)
    return x
```

```python
import functools

import jax
import jax.numpy as jnp
from jax import lax
from jax.experimental import pallas as pl
from jax.experimental.pallas import tpu as pltpu

_EPS = 1e-6
_LANES = 128
_SUBLANES = 8
_CF_HALO = 32
_SC_HALO = 8
_ROWS = 32
_VMEM_LIMIT = 56 * 1024 * 1024


def _row_loop(n_rows, chunk, body):
    def step(i, carry):
        body(pl.multiple_of(i * chunk, chunk))
        return carry
    lax.fori_loop(0, n_rows // chunk, step, 0)


def _rms_rows(x, g):
    ms = jnp.mean(x * x, axis=-1, keepdims=True)
    return x * lax.rsqrt(ms + _EPS) * g


def _const_spec(shape):
    nd = len(shape)
    return pl.BlockSpec(shape, lambda *_: (0,) * nd, pipeline_mode=pl.Buffered(1))


def _mix_kernel(x_ref, g_ref, win_ref, b1_ref, scw_ref, scwo_ref, cfw_ref, cfcb_ref, lng_ref, lnb_ref,
                pw2_ref, b2_ref, wmo_ref, o_ref, hb, t0, t1, cfbuf, scbuf, act, macc, *, sc_k, cf_k):
    T, D = x_ref.shape
    nlb = D // _LANES
    s = pl.program_id(1)

    @pl.when(s == 0)
    def _():
        cfbuf[:, 0:_CF_HALO, :] = jnp.zeros((nlb, _CF_HALO, _LANES), jnp.float32)
        scbuf[:, 0:_SC_HALO, :] = jnp.zeros((nlb, _SC_HALO, _LANES), jnp.float32)

    def norm_rows(r0):
        rows = pl.ds(r0, _ROWS)
        hb[rows, :] = _rms_rows(x_ref[rows, :], g_ref[...]).astype(jnp.bfloat16)
    _row_loop(T, _ROWS, norm_rows)

    def proj(col):
        return jnp.dot(hb[...], win_ref[:, col * D:(col + 1) * D], preferred_element_type=jnp.float32)

    t0[...] = proj(3)
    t1[...] = proj(4)

    def glu_rows(r0):
        rows = pl.ds(r0, _ROWS)
        a = t0[rows, :] + b1_ref[:, 0:D]
        g = t1[rows, :] + b1_ref[:, D:2 * D]
        v = a * jax.nn.sigmoid(g)
        for c in range(nlb):
            cfbuf[c, pl.ds(_CF_HALO + r0, _ROWS), :] = v[:, c * _LANES:(c + 1) * _LANES]
    _row_loop(T, _ROWS, glu_rows)

    def cf_rows(r0):
        rows = pl.ds(r0, _ROWS)
        accs = []
        for c in range(nlb):
            lanes = slice(c * _LANES, (c + 1) * _LANES)
            acc = jnp.broadcast_to(cfcb_ref[:, lanes], (_ROWS, _LANES))
            for k in range(cf_k):
                acc = acc + cfw_ref[k:k + 1, lanes] * cfbuf[c, pl.ds(r0 + _CF_HALO - (cf_k - 1) + k, _ROWS), :]
            accs.append(acc)
        tot = accs[0]
        for c in range(1, nlb):
            tot = tot + accs[c]
        mu = jnp.sum(tot, axis=-1, keepdims=True) * (1.0 / D)
        devs = [a - mu for a in accs]
        sq = devs[0] * devs[0]
        for c in range(1, nlb):
            sq = sq + devs[c] * devs[c]
        rstd = lax.rsqrt(jnp.sum(sq, axis=-1, keepdims=True) * (1.0 / D) + _EPS)
        for c in range(nlb):
            lanes = slice(c * _LANES, (c + 1) * _LANES)
            y = devs[c] * rstd * lng_ref[:, lanes] + lnb_ref[:, lanes]
            act[rows, lanes] = (y * jax.nn.sigmoid(y)).astype(jnp.bfloat16)
    _row_loop(T, _ROWS, cf_rows)

    t0[...] = jnp.dot(act[...], pw2_ref[...], preferred_element_type=jnp.float32)
    t1[...] = proj(6)

    def gate_cf_rows(r0):
        rows = pl.ds(r0, _ROWS)
        macc[rows, :] = jax.nn.sigmoid(t1[rows, :]) * (t0[rows, :] + b2_ref[...])
    _row_loop(T, _ROWS, gate_cf_rows)

    t0[...] = proj(1)
    t1[...] = proj(2)

    def cx_rows(r0):
        rows = pl.ds(r0, _ROWS)
        v = t0[rows, :] * t1[rows, :]
        for c in range(nlb):
            scbuf[c, pl.ds(_SC_HALO + r0, _ROWS), :] = v[:, c * _LANES:(c + 1) * _LANES]
    _row_loop(T, _ROWS, cx_rows)

    t0[...] = proj(0)

    def sc_rows(r0):
        rows = pl.ds(r0, _ROWS)
        for c in range(nlb):
            lanes = slice(c * _LANES, (c + 1) * _LANES)
            acc = scw_ref[0:1, lanes] * scbuf[c, pl.ds(r0 + _SC_HALO - (sc_k - 1), _ROWS), :]
            for k in range(1, sc_k):
                acc = acc + scw_ref[k:k + 1, lanes] * scbuf[c, pl.ds(r0 + _SC_HALO - (sc_k - 1) + k, _ROWS), :]
            act[rows, lanes] = (t0[rows, lanes] * acc).astype(jnp.bfloat16)
    _row_loop(T, _ROWS, sc_rows)

    t0[...] = jnp.dot(act[...], scwo_ref[...], preferred_element_type=jnp.float32)
    t1[...] = proj(5)

    def merge_rows(r0):
        rows = pl.ds(r0, _ROWS)
        m = macc[rows, :] + jax.nn.sigmoid(t1[rows, :]) * t0[rows, :]
        act[rows, :] = m.astype(jnp.bfloat16)
    _row_loop(T, _ROWS, merge_rows)

    o_ref[...] = x_ref[...] + jnp.dot(act[...], wmo_ref[...], preferred_element_type=jnp.float32)

    cfbuf[:, 0:_CF_HALO, :] = cfbuf[:, T:T + _CF_HALO, :]
    scbuf[:, 0:_SC_HALO, :] = scbuf[:, T:T + _SC_HALO, :]


def _conv_mixers(x, g, w_in, b1, sc_w, sc_wo, cf_w, cf_cb, ln_g, ln_b, pw2, b2, w_mo, *, tile):
    B, S, D = x.shape
    T = min(tile, S)
    assert S % T == 0 and T % _ROWS == 0 and D % _LANES == 0 and T >= _CF_HALO
    sc_k, cf_k = sc_w.shape[0], cf_w.shape[0]
    assert cf_k - 1 <= _CF_HALO and sc_k - 1 <= _SC_HALO
    nlb = D // _LANES
    bf = jnp.bfloat16
    row = lambda v: v.reshape(1, -1)
    kern = functools.partial(_mix_kernel, sc_k=sc_k, cf_k=cf_k)
    xspec = pl.BlockSpec((None, T, D), lambda b, s: (b, s, 0))
    args = (x, row(g), w_in.astype(bf), row(b1), sc_w, sc_wo.astype(bf), cf_w, row(cf_cb), row(ln_g), row(ln_b),
            pw2.astype(bf), row(b2), w_mo.astype(bf))
    return pl.pallas_call(
        kern,
        out_shape=jax.ShapeDtypeStruct((B, S, D), jnp.float32),
        grid=(B, S // T),
        in_specs=[xspec] + [_const_spec(a.shape) for a in args[1:]],
        out_specs=xspec,
        scratch_shapes=[
            pltpu.VMEM((T, D), bf),
            pltpu.VMEM((T, D), jnp.float32),
            pltpu.VMEM((T, D), jnp.float32),
            pltpu.VMEM((nlb, _CF_HALO + T, _LANES), jnp.float32),
            pltpu.VMEM((nlb, _SC_HALO + T, _LANES), jnp.float32),
            pltpu.VMEM((T, D), bf),
            pltpu.VMEM((T, D), jnp.float32),
        ],
        compiler_params=pltpu.CompilerParams(
            dimension_semantics=("arbitrary", "arbitrary"), vmem_limit_bytes=_VMEM_LIMIT),
        name="conv_mixers",
    )(*args)


def _memkv_kernel(mem_ref, g_ref, wkv_ref, k_ref, v_ref):
    E = k_ref.shape[-1]
    mn = _rms_rows(mem_ref[...], g_ref[...]).astype(jnp.bfloat16)
    kv = jnp.dot(mn, wkv_ref[...], preferred_element_type=jnp.float32)
    k_ref[...] = kv[:, 0:E].astype(jnp.bfloat16)
    v_ref[...] = kv[:, E:2 * E].astype(jnp.bfloat16)


def _mem_kv(mem, g, w_kv):
    B, M, D = mem.shape
    E = w_kv.shape[1] // 2
    bf = jnp.bfloat16
    out = jax.ShapeDtypeStruct((B, M, E), bf)
    ospec = pl.BlockSpec((None, M, E), lambda b: (b, 0, 0))
    return pl.pallas_call(
        _memkv_kernel,
        out_shape=(out, out),
        grid=(B,),
        in_specs=[pl.BlockSpec((None, M, D), lambda b: (b, 0, 0)), _const_spec((1, D)), _const_spec(w_kv.shape)],
        out_specs=(ospec, ospec),
        compiler_params=pltpu.CompilerParams(dimension_semantics=("arbitrary",), vmem_limit_bytes=_VMEM_LIMIT),
        name="mem_kv",
    )(mem, g.reshape(1, D), w_kv.astype(bf))


def _xattn_kernel(x_ref, g_ref, wq_ref, k_ref, v_ref, wo_ref, o_ref, hb, ob, *, heads):
    T, D = x_ref.shape
    E = wq_ref.shape[1]
    hd = E // heads

    def norm_rows(r0):
        rows = pl.ds(r0, _ROWS)
        hb[rows, :] = _rms_rows(x_ref[rows, :], g_ref[...]).astype(jnp.bfloat16)
    _row_loop(T, _ROWS, norm_rows)

    q = jnp.dot(hb[...], wq_ref[...], preferred_element_type=jnp.float32).astype(jnp.bfloat16)
    for h in range(heads):
        cols = slice(h * hd, (h + 1) * hd)
        sc = lax.dot_general(q[:, cols], k_ref[:, cols], (((1,), (1,)), ((), ())),
                             preferred_element_type=jnp.float32) * (hd ** -0.5)
        sc = sc - jnp.max(sc, axis=-1, keepdims=True)
        p = jnp.exp(sc)
        p = p / jnp.sum(p, axis=-1, keepdims=True)
        ob[:, cols] = jnp.dot(p.astype(jnp.bfloat16), v_ref[:, cols],
                              preferred_element_type=jnp.float32).astype(jnp.bfloat16)
    o_ref[...] = x_ref[...] + jnp.dot(ob[...], wo_ref[...], preferred_element_type=jnp.float32)


def _cross_attention(x, g, w_q, k, v, w_o, *, heads, tile):
    B, S, D = x.shape
    M, E = k.shape[1], k.shape[2]
    T = min(tile, S)
    assert S % T == 0 and T % _ROWS == 0
    bf = jnp.bfloat16
    xspec = pl.BlockSpec((None, T, D), lambda b, s: (b, s, 0))
    kvspec = pl.BlockSpec((None, M, E), lambda b, s: (b, 0, 0))
    return pl.pallas_call(
        functools.partial(_xattn_kernel, heads=heads),
        out_shape=jax.ShapeDtypeStruct((B, S, D), jnp.float32),
        grid=(B, S // T),
        in_specs=[xspec, _const_spec((1, D)), _const_spec(w_q.shape), kvspec, kvspec, _const_spec(w_o.shape)],
        out_specs=xspec,
        scratch_shapes=[pltpu.VMEM((T, D), bf), pltpu.VMEM((T, E), bf)],
        compiler_params=pltpu.CompilerParams(
            dimension_semantics=("arbitrary", "arbitrary"), vmem_limit_bytes=_VMEM_LIMIT),
        name="mem_cross_attention",
    )(x, g.reshape(1, D), w_q.astype(bf), k, v, w_o.astype(bf))


def _batcher_pairs(n):
    pairs = []
    p = 1
    while p < n:
        k = p
        while k >= 1:
            for j in range(k % p, n - k, 2 * k):
                for i in range(min(k, n - j - k)):
                    if (i + j) // (2 * p) == (i + j + k) // (2 * p):
                        pairs.append((i + j, i + j + k))
            k //= 2
        p *= 2
    return pairs


def _sort_desc(xs):
    xs = list(xs)
    for i, j in _batcher_pairs(len(xs)):
        hi, lo = jnp.maximum(xs[i], xs[j]), jnp.minimum(xs[i], xs[j])
        xs[i], xs[j] = hi, lo
    return xs


def _bitonic_sort_desc(xs):
    xs = list(xs)
    n = len(xs)
    stride = n // 2
    while stride >= 1:
        for i in range(n):
            if i & stride == 0:
                hi, lo = jnp.maximum(xs[i], xs[i + stride]), jnp.minimum(xs[i], xs[i + stride])
                xs[i], xs[i + stride] = hi, lo
        stride //= 2
    return xs


def _sublane_partner(xs, shift):
    return [pltpu.roll(x, shift, 0) for x in xs]


def _top_sorted(score, k):
    n = score.shape[0]
    assert n == k * _SUBLANES
    xs = _sort_desc([score[g * _SUBLANES:(g + 1) * _SUBLANES, :] for g in range(k)])
    for shift in (4, 2, 1):
        ps = _sublane_partner(xs, shift)
        xs = _bitonic_sort_desc([jnp.maximum(xs[r], ps[k - 1 - r]) for r in range(k)])
    return xs


def _pair_threshold(a, b, k):
    assert k == 16
    neg = jnp.float32(-jnp.inf)
    sub = lax.broadcasted_iota(jnp.int32, a[0].shape, 0)

    def stack(vals):
        out = vals[-1]
        for s in range(len(vals) - 2, -1, -1):
            out = jnp.where(sub == s, vals[s], out)
        return out

    a_lo, a_hi = stack(a[0:8]), stack(a[8:16])
    b_lo, b_hi = stack(b[0:8]), stack(b[8:16])
    mid = sub >= 2
    cands = [
        a[0] + b_lo, a[0] + b_hi, a[1] + b_lo, a_hi + b[0],
        jnp.where(mid, a_lo + b[0], neg), jnp.where(mid, a_lo + b[1], neg),
        jnp.where(sub < 6,
                  stack([a[2], a[2], a[2], a[3], a[3], a[4], a[4], a[4]])
                  + stack([b[2], b[3], b[4], b[2], b[3], b[2], b[2], b[2]]), neg),
        jnp.full(a[0].shape, neg, jnp.float32),
    ]
    xs = _sort_desc(cands)
    ps = _sublane_partner(xs, 4)
    hi = _bitonic_sort_desc([jnp.maximum(xs[r], ps[7 - r]) for r in range(8)])
    lo = _bitonic_sort_desc([jnp.minimum(xs[r], ps[7 - r]) for r in range(8)])
    xs = hi + lo
    ps = _sublane_partner(xs, 2)
    xs = _bitonic_sort_desc([jnp.maximum(xs[r], ps[k - 1 - r]) for r in range(k)])
    ps = _sublane_partner(xs, 1)
    tau = jnp.maximum(xs[0], ps[k - 1])
    for r in range(1, k):
        tau = jnp.minimum(tau, jnp.maximum(xs[r], ps[k - 1 - r]))
    top = a[0] + b[0]
    z = jnp.zeros_like(tau)
    for c in cands[:-1]:
        z = z + jnp.where(c >= tau, jnp.exp(c - top), 0.0)
    z = jnp.sum(z, axis=0, keepdims=True)
    return tau, jnp.broadcast_to(z, tau.shape)


_PK = 16


def _pack_rows(lo, hi):
    return jnp.concatenate([lo, hi], axis=0).astype(jnp.bfloat16)


def _rank_among(s, b):
    r = jnp.full(s.shape, float(len(b)), jnp.float32)
    for k in range(len(b) - 1, -1, -1):
        r = jnp.where(s >= b[k], float(k), r)
    return r


def _pair_count(s, b, tau):
    c = jnp.zeros(s.shape, jnp.float32)
    for k in range(len(b)):
        c = jnp.where(s + b[k] >= tau, float(k + 1), c)
    return c


def _peer_kernel(x_ref, g_ref, wq_ref, sk_ref, u_ref, vt_ref, fg_ref, o_ref,
                 ht, cnts, cps, r1s, e1s, acc, st, wbuf, at, *, heads, topk, ehalf):
    T, D = x_ref.shape
    et = u_ref.shape[0]
    nk = sk_ref.shape[1]
    half = sk_ref.shape[2]
    j = pl.program_id(1)
    bf = jnp.bfloat16

    @pl.when(j == 0)
    def _route():
        h = _rms_rows(x_ref[...], g_ref[...])
        ht[...] = h.T.astype(bf)
        acc[...] = jnp.zeros_like(acc)
        for hd in range(heads):
            q = jnp.dot(wq_ref[hd * 2 * half:(hd + 1) * 2 * half, :], ht[...],
                        preferred_element_type=jnp.float32).astype(bf)
            s0 = jnp.dot(sk_ref[0], q[0:half, :], preferred_element_type=jnp.float32)
            s1 = jnp.dot(sk_ref[1], q[half:2 * half, :], preferred_element_type=jnp.float32)
            a = _top_sorted(s0, topk)
            b = _top_sorted(s1, topk)
            tau, z = _pair_threshold(a, b, topk)
            for gq in range(nk // _SUBLANES):
                rows = slice(gq * _SUBLANES, (gq + 1) * _SUBLANES)
                cnts[hd, rows, :] = _pair_count(s0[rows, :], b, tau)
                cps[hd, rows, :] = jnp.exp(s0[rows, :] - a[0]) * ((2.0 ** -0.5) / z)
            for gq in range(nk // _PK):
                rows = slice(gq * _PK, (gq + 1) * _PK)
                lo = s1[gq * _PK:gq * _PK + _SUBLANES, :]
                hi = s1[gq * _PK + _SUBLANES:(gq + 1) * _PK, :]
                r1s[hd, rows, :] = _pack_rows(_rank_among(lo, b), _rank_among(hi, b))
                e1s[hd, rows, :] = _pack_rows(jnp.exp(lo - b[0]), jnp.exp(hi - b[0]))

    n_sub = et // ehalf
    per = ehalf // nk
    for sb in range(n_sub):
        for ii in range(per):
            i = (j * n_sub + sb) * per + ii
            cnt_rows = [cnts[hd, pl.ds(i, 1), :] for hd in range(heads)]
            cp_rows = [cps[hd, pl.ds(i, 1), :] for hd in range(heads)]
            for lt in range(T // _LANES):
                lanes = slice(lt * _LANES, (lt + 1) * _LANES)
                w = [jnp.zeros((_PK, _LANES), bf) for _ in range(nk // _PK)]
                for hd in range(heads):
                    cnt = jnp.broadcast_to(cnt_rows[hd][:, lanes], (_SUBLANES, _LANES))
                    cp = jnp.broadcast_to(cp_rows[hd][:, lanes], (_SUBLANES, _LANES))
                    cnt = _pack_rows(cnt, cnt)
                    cp = _pack_rows(cp, cp)
                    for gq in range(nk // _PK):
                        rows = slice(gq * _PK, (gq + 1) * _PK)
                        e1 = e1s[hd, rows, lanes]
                        w[gq] = w[gq] + jnp.where(r1s[hd, rows, lanes] < cnt, e1, jnp.zeros_like(e1)) * cp
                for gq in range(nk // _PK):
                    wbuf[sb, ii * nk + gq * _PK:ii * nk + (gq + 1) * _PK, lanes] = w[gq]
    for sb in range(n_sub):
        st[sb] = jnp.dot(u_ref[sb * ehalf:(sb + 1) * ehalf, :], ht[...], preferred_element_type=jnp.float32)
    for sb in range(n_sub):
        for gq in range(ehalf // _PK):
            rows = slice(gq * _PK, (gq + 1) * _PK)
            z = st[sb, rows, :] * (2.0 ** -0.5)
            g = z * (1.0 + lax.erf(z))
            at[sb, rows, :] = _pack_rows(g[0:_SUBLANES, :], g[_SUBLANES:_PK, :]) * wbuf[sb, rows, :]
    out = jnp.dot(vt_ref[:, 0:ehalf], at[0], preferred_element_type=jnp.float32)
    for sb in range(1, n_sub):
        out = out + jnp.dot(vt_ref[:, sb * ehalf:(sb + 1) * ehalf], at[sb], preferred_element_type=jnp.float32)
    acc[...] += out

    @pl.when(j == pl.num_programs(1) - 1)
    def _finish():
        y = x_ref[...] + acc[...].T
        o_ref[...] = _rms_rows(y, fg_ref[...])


def _peer_final(x, g, w_pq, subkeys, u, v, final_g, *, heads, topk, tile, etile):
    N, D = x.shape
    ne = u.shape[0]
    nk, half = subkeys.shape[1], subkeys.shape[2]
    T = min(tile, N)
    assert N % T == 0 and T % _LANES == 0 and ne == nk * nk and ne % etile == 0 and etile % nk == 0
    assert nk == topk * _SUBLANES and w_pq.shape[1] == heads * 2 * half
    ehalf = min(etile, 2 * nk)
    assert ehalf % nk == 0 and etile % ehalf == 0
    bf = jnp.bfloat16
    f32 = jnp.float32
    n_et = ne // etile
    vt = v.astype(bf).reshape(n_et, etile, D).transpose(0, 2, 1)
    xspec = pl.BlockSpec((T, D), lambda i, j: (i, 0))
    return pl.pallas_call(
        functools.partial(_peer_kernel, heads=heads, topk=topk, ehalf=ehalf),
        out_shape=jax.ShapeDtypeStruct((N, D), f32),
        grid=(N // T, n_et),
        in_specs=[xspec, _const_spec((1, D)), _const_spec((w_pq.shape[1], D)), _const_spec(subkeys.shape),
                  pl.BlockSpec((etile, D), lambda i, j: (j, 0)),
                  pl.BlockSpec((None, D, etile), lambda i, j: (j, 0, 0)),
                  _const_spec((1, D))],
        out_specs=xspec,
        scratch_shapes=[
            pltpu.VMEM((D, T), bf),
            pltpu.VMEM((heads, nk, T), f32),
            pltpu.VMEM((heads, nk, T), f32),
            pltpu.VMEM((heads, nk, T), bf),
            pltpu.VMEM((heads, nk, T), bf),
            pltpu.VMEM((D, T), f32),
            pltpu.VMEM((etile // ehalf, ehalf, T), f32),
            pltpu.VMEM((etile // ehalf, ehalf, T), bf),
            pltpu.VMEM((etile // ehalf, ehalf, T), bf),
        ],
        compiler_params=pltpu.CompilerParams(
            dimension_semantics=("arbitrary", "arbitrary"), vmem_limit_bytes=_VMEM_LIMIT),
        name="peer_final",
    )(x, g.reshape(1, D), w_pq.T.astype(bf), subkeys.astype(bf), u.astype(bf), vt, final_g.reshape(1, D))


def kernel(x, mem, norm_mix_g, w_in, cf_b_pw1, sc_conv_w, sc_w_out, cf_conv_w, cf_conv_b, cf_ln_g, cf_ln_b, cf_w_pw2, cf_b_pw2, w_mix_out, norm_xa_g, norm_mem_g, w_q, w_kv, w_xo, norm_peer_g, w_peer_q, peer_subkeys, peer_u, peer_v, final_norm_g):
    B, S, D = x.shape
    depth = norm_mix_g.shape[0]
    heads_xa = 4
    heads_peer = 8
    topk = 16
    for l in range(depth):
        x = _conv_mixers(x, norm_mix_g[l], w_in[l], cf_b_pw1[l], sc_conv_w[l], sc_w_out[l], cf_conv_w[l], cf_conv_b[l],
                         cf_ln_g[l], cf_ln_b[l], cf_w_pw2[l], cf_b_pw2[l], w_mix_out[l], tile=512)
        k, v = _mem_kv(mem, norm_mem_g[l], w_kv[l])
        x = _cross_attention(x, norm_xa_g[l], w_q[l], k, v, w_xo[l], heads=heads_xa, tile=512)
        last = l == depth - 1
        fg = final_norm_g if last else None
        assert last, "final RMSNorm is fused into the last PEER call"
        x = _peer_final(x.reshape(B * S, D), norm_peer_g[l], w_peer_q[l], peer_subkeys[l], peer_u[l], peer_v[l], fg,
                        heads=heads_peer, topk=topk, tile=512, etile=2048).reshape(B, S, D)
    return x
```

```python
import functools

import jax
import jax.numpy as jnp
from jax import lax
from jax.experimental import pallas as pl
from jax.experimental.pallas import tpu as pltpu

_EPS = 1e-6
_LANES = 128
_SUBLANES = 8
_CF_HALO = 32
_SC_HALO = 8
_ROWS = 32
_CONV_ROWS = 64
_VMEM_LIMIT = 56 * 1024 * 1024


def _row_loop(n_rows, chunk, body, unroll=1):
    def step(i, carry):
        body(pl.multiple_of(i * chunk, chunk))
        return carry
    lax.fori_loop(0, n_rows // chunk, step, 0, unroll=unroll)


def _rms_rows(x, g):
    ms = jnp.mean(x * x, axis=-1, keepdims=True)
    return x * lax.rsqrt(ms + _EPS) * g


def _const_spec(shape):
    nd = len(shape)
    return pl.BlockSpec(shape, lambda *_: (0,) * nd, pipeline_mode=pl.Buffered(1))


def _mix_kernel(x_ref, g_ref, win_ref, b1_ref, scw_ref, scwo_ref, cfw_ref, cfcb_ref, lng_ref, lnb_ref,
                pw2_ref, b2_ref, wmo_ref, o_ref, hb, t0, t1, cfbuf, cfout, scbuf, act, macc, *, sc_k, cf_k):
    T, D = x_ref.shape
    nlb = D // _LANES
    s = pl.program_id(1)

    @pl.when(s == 0)
    def _():
        cfbuf[:, 0:_CF_HALO, :] = jnp.zeros((nlb, _CF_HALO, _LANES), jnp.float32)
        scbuf[:, 0:_SC_HALO, :] = jnp.zeros((nlb, _SC_HALO, _LANES), jnp.float32)

    def norm_rows(r0):
        rows = pl.ds(r0, _ROWS)
        hb[rows, :] = _rms_rows(x_ref[rows, :], g_ref[...]).astype(jnp.bfloat16)
    _row_loop(T, _ROWS, norm_rows)

    def proj(col):
        return jnp.dot(hb[...], win_ref[:, col * D:(col + 1) * D], preferred_element_type=jnp.float32)

    t0[...] = proj(3)
    t1[...] = proj(4)

    def glu_rows(r0):
        rows = pl.ds(r0, _ROWS)
        a = t0[rows, :] + b1_ref[:, 0:D]
        g = t1[rows, :] + b1_ref[:, D:2 * D]
        v = a * jax.nn.sigmoid(g)
        for c in range(nlb):
            cfbuf[c, pl.ds(_CF_HALO + r0, _ROWS), :] = v[:, c * _LANES:(c + 1) * _LANES]
    _row_loop(T, _ROWS, glu_rows)

    def conv_rows(r0):
        for c in range(nlb):
            lanes = slice(c * _LANES, (c + 1) * _LANES)
            acc = jnp.broadcast_to(cfcb_ref[:, lanes], (_CONV_ROWS, _LANES))
            for k in range(cf_k):
                acc = acc + cfw_ref[k:k + 1, lanes] * cfbuf[c, pl.ds(r0 + _CF_HALO - (cf_k - 1) + k, _CONV_ROWS), :]
            cfout[c, pl.ds(r0, _CONV_ROWS), :] = acc
    _row_loop(T, _CONV_ROWS, conv_rows)

    def cf_rows(r0):
        rows = pl.ds(r0, _ROWS)
        accs = [cfout[c, rows, :] for c in range(nlb)]
        tot = accs[0]
        for c in range(1, nlb):
            tot = tot + accs[c]
        mu = jnp.sum(tot, axis=-1, keepdims=True) * (1.0 / D)
        devs = [a - mu for a in accs]
        sq = devs[0] * devs[0]
        for c in range(1, nlb):
            sq = sq + devs[c] * devs[c]
        rstd = lax.rsqrt(jnp.sum(sq, axis=-1, keepdims=True) * (1.0 / D) + _EPS)
        for c in range(nlb):
            lanes = slice(c * _LANES, (c + 1) * _LANES)
            y = devs[c] * rstd * lng_ref[:, lanes] + lnb_ref[:, lanes]
            act[rows, lanes] = (y * jax.nn.sigmoid(y)).astype(jnp.bfloat16)
    _row_loop(T, _ROWS, cf_rows, unroll=2)

    t0[...] = jnp.dot(act[...], pw2_ref[...], preferred_element_type=jnp.float32)
    t1[...] = proj(6)

    def gate_cf_rows(r0):
        rows = pl.ds(r0, _ROWS)
        macc[rows, :] = jax.nn.sigmoid(t1[rows, :]) * (t0[rows, :] + b2_ref[...])
    _row_loop(T, _ROWS, gate_cf_rows)

    t0[...] = proj(1)
    t1[...] = proj(2)

    def cx_rows(r0):
        rows = pl.ds(r0, _ROWS)
        v = t0[rows, :] * t1[rows, :]
        for c in range(nlb):
            scbuf[c, pl.ds(_SC_HALO + r0, _ROWS), :] = v[:, c * _LANES:(c + 1) * _LANES]
    _row_loop(T, _ROWS, cx_rows)

    t0[...] = proj(0)

    def sc_rows(r0):
        rows = pl.ds(r0, _ROWS)
        for c in range(nlb):
            lanes = slice(c * _LANES, (c + 1) * _LANES)
            acc = scw_ref[0:1, lanes] * scbuf[c, pl.ds(r0 + _SC_HALO - (sc_k - 1), _ROWS), :]
            for k in range(1, sc_k):
                acc = acc + scw_ref[k:k + 1, lanes] * scbuf[c, pl.ds(r0 + _SC_HALO - (sc_k - 1) + k, _ROWS), :]
            act[rows, lanes] = (t0[rows, lanes] * acc).astype(jnp.bfloat16)
    _row_loop(T, _ROWS, sc_rows)

    t0[...] = jnp.dot(act[...], scwo_ref[...], preferred_element_type=jnp.float32)
    t1[...] = proj(5)

    def merge_rows(r0):
        rows = pl.ds(r0, _ROWS)
        m = macc[rows, :] + jax.nn.sigmoid(t1[rows, :]) * t0[rows, :]
        act[rows, :] = m.astype(jnp.bfloat16)
    _row_loop(T, _ROWS, merge_rows)

    o_ref[...] = x_ref[...] + jnp.dot(act[...], wmo_ref[...], preferred_element_type=jnp.float32)

    cfbuf[:, 0:_CF_HALO, :] = cfbuf[:, T:T + _CF_HALO, :]
    scbuf[:, 0:_SC_HALO, :] = scbuf[:, T:T + _SC_HALO, :]


def _conv_mixers(x, g, w_in, b1, sc_w, sc_wo, cf_w, cf_cb, ln_g, ln_b, pw2, b2, w_mo, *, tile):
    B, S, D = x.shape
    T = min(tile, S)
    assert S % T == 0 and T % _ROWS == 0 and T % _CONV_ROWS == 0 and D % _LANES == 0 and T >= _CF_HALO
    sc_k, cf_k = sc_w.shape[0], cf_w.shape[0]
    assert cf_k - 1 <= _CF_HALO and sc_k - 1 <= _SC_HALO
    nlb = D // _LANES
    bf = jnp.bfloat16
    row = lambda v: v.reshape(1, -1)
    kern = functools.partial(_mix_kernel, sc_k=sc_k, cf_k=cf_k)
    xspec = pl.BlockSpec((None, T, D), lambda b, s: (b, s, 0))
    args = (x, row(g), w_in.astype(bf), row(b1), sc_w, sc_wo.astype(bf), cf_w, row(cf_cb), row(ln_g), row(ln_b),
            pw2.astype(bf), row(b2), w_mo.astype(bf))
    return pl.pallas_call(
        kern,
        out_shape=jax.ShapeDtypeStruct((B, S, D), jnp.float32),
        grid=(B, S // T),
        in_specs=[xspec] + [_const_spec(a.shape) for a in args[1:]],
        out_specs=xspec,
        scratch_shapes=[
            pltpu.VMEM((T, D), bf),
            pltpu.VMEM((T, D), jnp.float32),
            pltpu.VMEM((T, D), jnp.float32),
            pltpu.VMEM((nlb, _CF_HALO + T, _LANES), jnp.float32),
            pltpu.VMEM((nlb, T, _LANES), jnp.float32),
            pltpu.VMEM((nlb, _SC_HALO + T, _LANES), jnp.float32),
            pltpu.VMEM((T, D), bf),
            pltpu.VMEM((T, D), jnp.float32),
        ],
        compiler_params=pltpu.CompilerParams(
            dimension_semantics=("arbitrary", "arbitrary"), vmem_limit_bytes=_VMEM_LIMIT),
        name="conv_mixers",
    )(*args)


def _memkv_kernel(mem_ref, g_ref, wkv_ref, k_ref, v_ref):
    E = k_ref.shape[-1]
    mn = _rms_rows(mem_ref[...], g_ref[...]).astype(jnp.bfloat16)
    kv = jnp.dot(mn, wkv_ref[...], preferred_element_type=jnp.float32)
    k_ref[...] = kv[:, 0:E].astype(jnp.bfloat16)
    v_ref[...] = kv[:, E:2 * E].astype(jnp.bfloat16)


def _mem_kv(mem, g, w_kv):
    B, M, D = mem.shape
    E = w_kv.shape[1] // 2
    bf = jnp.bfloat16
    out = jax.ShapeDtypeStruct((B, M, E), bf)
    ospec = pl.BlockSpec((None, M, E), lambda b: (b, 0, 0))
    return pl.pallas_call(
        _memkv_kernel,
        out_shape=(out, out),
        grid=(B,),
        in_specs=[pl.BlockSpec((None, M, D), lambda b: (b, 0, 0)), _const_spec((1, D)), _const_spec(w_kv.shape)],
        out_specs=(ospec, ospec),
        compiler_params=pltpu.CompilerParams(dimension_semantics=("arbitrary",), vmem_limit_bytes=_VMEM_LIMIT),
        name="mem_kv",
    )(mem, g.reshape(1, D), w_kv.astype(bf))


def _xattn_kernel(x_ref, g_ref, wq_ref, k_ref, v_ref, wo_ref, o_ref, hb, ob, *, heads):
    T, D = x_ref.shape
    E = wq_ref.shape[1]
    hd = E // heads

    def norm_rows(r0):
        rows = pl.ds(r0, _ROWS)
        hb[rows, :] = _rms_rows(x_ref[rows, :], g_ref[...]).astype(jnp.bfloat16)
    _row_loop(T, _ROWS, norm_rows)

    q = jnp.dot(hb[...], wq_ref[...], preferred_element_type=jnp.float32).astype(jnp.bfloat16)
    for h in range(heads):
        cols = slice(h * hd, (h + 1) * hd)
        sc = lax.dot_general(q[:, cols], k_ref[:, cols], (((1,), (1,)), ((), ())),
                             preferred_element_type=jnp.float32) * (hd ** -0.5)
        sc = sc - jnp.max(sc, axis=-1, keepdims=True)
        p = jnp.exp(sc)
        p = p / jnp.sum(p, axis=-1, keepdims=True)
        ob[:, cols] = jnp.dot(p.astype(jnp.bfloat16), v_ref[:, cols],
                              preferred_element_type=jnp.float32).astype(jnp.bfloat16)
    o_ref[...] = x_ref[...] + jnp.dot(ob[...], wo_ref[...], preferred_element_type=jnp.float32)


def _cross_attention(x, g, w_q, k, v, w_o, *, heads, tile):
    B, S, D = x.shape
    M, E = k.shape[1], k.shape[2]
    T = min(tile, S)
    assert S % T == 0 and T % _ROWS == 0
    bf = jnp.bfloat16
    xspec = pl.BlockSpec((None, T, D), lambda b, s: (b, s, 0))
    kvspec = pl.BlockSpec((None, M, E), lambda b, s: (b, 0, 0))
    return pl.pallas_call(
        functools.partial(_xattn_kernel, heads=heads),
        out_shape=jax.ShapeDtypeStruct((B, S, D), jnp.float32),
        grid=(B, S // T),
        in_specs=[xspec, _const_spec((1, D)), _const_spec(w_q.shape), kvspec, kvspec, _const_spec(w_o.shape)],
        out_specs=xspec,
        scratch_shapes=[pltpu.VMEM((T, D), bf), pltpu.VMEM((T, E), bf)],
        compiler_params=pltpu.CompilerParams(
            dimension_semantics=("arbitrary", "arbitrary"), vmem_limit_bytes=_VMEM_LIMIT),
        name="mem_cross_attention",
    )(x, g.reshape(1, D), w_q.astype(bf), k, v, w_o.astype(bf))


def _batcher_pairs(n):
    pairs = []
    p = 1
    while p < n:
        k = p
        while k >= 1:
            for j in range(k % p, n - k, 2 * k):
                for i in range(min(k, n - j - k)):
                    if (i + j) // (2 * p) == (i + j + k) // (2 * p):
                        pairs.append((i + j, i + j + k))
            k //= 2
        p *= 2
    return pairs


def _sort_desc(xs):
    xs = list(xs)
    for i, j in _batcher_pairs(len(xs)):
        hi, lo = jnp.maximum(xs[i], xs[j]), jnp.minimum(xs[i], xs[j])
        xs[i], xs[j] = hi, lo
    return xs


def _bitonic_sort_desc(xs):
    xs = list(xs)
    n = len(xs)
    stride = n // 2
    while stride >= 1:
        for i in range(n):
            if i & stride == 0:
                hi, lo = jnp.maximum(xs[i], xs[i + stride]), jnp.minimum(xs[i], xs[i + stride])
                xs[i], xs[i + stride] = hi, lo
        stride //= 2
    return xs


def _sublane_partner(xs, shift):
    return [pltpu.roll(x, shift, 0) for x in xs]


def _top_sorted(score, k):
    n = score.shape[0]
    assert n == k * _SUBLANES
    xs = _sort_desc([score[g * _SUBLANES:(g + 1) * _SUBLANES, :] for g in range(k)])
    for shift in (4, 2, 1):
        ps = _sublane_partner(xs, shift)
        xs = _bitonic_sort_desc([jnp.maximum(xs[r], ps[k - 1 - r]) for r in range(k)])
    return xs


def _pair_threshold(a, b, k):
    assert k == 16
    neg = jnp.float32(-jnp.inf)
    sub = lax.broadcasted_iota(jnp.int32, a[0].shape, 0)

    def stack(vals):
        out = vals[-1]
        for s in range(len(vals) - 2, -1, -1):
            out = jnp.where(sub == s, vals[s], out)
        return out

    a_lo, a_hi = stack(a[0:8]), stack(a[8:16])
    b_lo, b_hi = stack(b[0:8]), stack(b[8:16])
    mid = sub >= 2
    cands = [
        a[0] + b_lo, a[0] + b_hi, a[1] + b_lo, a_hi + b[0],
        jnp.where(mid, a_lo + b[0], neg), jnp.where(mid, a_lo + b[1], neg),
        jnp.where(sub < 6,
                  stack([a[2], a[2], a[2], a[3], a[3], a[4], a[4], a[4]])
                  + stack([b[2], b[3], b[4], b[2], b[3], b[2], b[2], b[2]]), neg),
        jnp.full(a[0].shape, neg, jnp.float32),
    ]
    xs = _sort_desc(cands)
    ps = _sublane_partner(xs, 4)
    hi = _bitonic_sort_desc([jnp.maximum(xs[r], ps[7 - r]) for r in range(8)])
    lo = _bitonic_sort_desc([jnp.minimum(xs[r], ps[7 - r]) for r in range(8)])
    xs = hi + lo
    ps = _sublane_partner(xs, 2)
    xs = _bitonic_sort_desc([jnp.maximum(xs[r], ps[k - 1 - r]) for r in range(k)])
    ps = _sublane_partner(xs, 1)
    tau = jnp.maximum(xs[0], ps[k - 1])
    for r in range(1, k):
        tau = jnp.minimum(tau, jnp.maximum(xs[r], ps[k - 1 - r]))
    top = a[0] + b[0]
    z = jnp.zeros_like(tau)
    for c in cands[:-1]:
        z = z + jnp.where(c >= tau, jnp.exp(c - top), 0.0)
    z = jnp.sum(z, axis=0, keepdims=True)
    return tau, jnp.broadcast_to(z, tau.shape)


_PK = 16


def _pack_rows(lo, hi):
    return jnp.concatenate([lo, hi], axis=0).astype(jnp.bfloat16)


def _rank_among(s, b):
    r = jnp.full(s.shape, float(len(b)), jnp.float32)
    for k in range(len(b) - 1, -1, -1):
        r = jnp.where(s >= b[k], float(k), r)
    return r


def _pair_count(s, b, tau):
    c = jnp.zeros(s.shape, jnp.float32)
    for k in range(len(b)):
        c = jnp.where(s + b[k] >= tau, float(k + 1), c)
    return c


def _peer_kernel(x_ref, g_ref, wq_ref, sk_ref, u_ref, vt_ref, fg_ref, o_ref,
                 ht, cnts, cps, r1s, e1s, acc, st, wbuf, at, *, heads, topk, ehalf):
    T, D = x_ref.shape
    et = u_ref.shape[0]
    nk = sk_ref.shape[1]
    half = sk_ref.shape[2]
    j = pl.program_id(1)
    bf = jnp.bfloat16

    @pl.when(j == 0)
    def _route():
        h = _rms_rows(x_ref[...], g_ref[...])
        ht[...] = h.T.astype(bf)
        acc[...] = jnp.zeros_like(acc)
        for hd in range(heads):
            q = jnp.dot(wq_ref[hd * 2 * half:(hd + 1) * 2 * half, :], ht[...],
                        preferred_element_type=jnp.float32).astype(bf)
            s0 = jnp.dot(sk_ref[0], q[0:half, :], preferred_element_type=jnp.float32)
            s1 = jnp.dot(sk_ref[1], q[half:2 * half, :], preferred_element_type=jnp.float32)
            a = _top_sorted(s0, topk)
            b = _top_sorted(s1, topk)
            tau, z = _pair_threshold(a, b, topk)
            for gq in range(nk // _SUBLANES):
                rows = slice(gq * _SUBLANES, (gq + 1) * _SUBLANES)
                cnts[hd, rows, :] = _pair_count(s0[rows, :], b, tau)
                cps[hd, rows, :] = jnp.exp(s0[rows, :] - a[0]) * ((2.0 ** -0.5) / z)
            for gq in range(nk // _PK):
                rows = slice(gq * _PK, (gq + 1) * _PK)
                lo = s1[gq * _PK:gq * _PK + _SUBLANES, :]
                hi = s1[gq * _PK + _SUBLANES:(gq + 1) * _PK, :]
                r1s[hd, rows, :] = _pack_rows(_rank_among(lo, b), _rank_among(hi, b))
                e1s[hd, rows, :] = _pack_rows(jnp.exp(lo - b[0]), jnp.exp(hi - b[0]))

    n_sub = et // ehalf
    per = ehalf // nk
    for sb in range(n_sub):
        for ii in range(per):
            i = (j * n_sub + sb) * per + ii
            w = [jnp.zeros((_PK, T), bf) for _ in range(nk // _PK)]
            for hd in range(heads):
                cnt = jnp.broadcast_to(cnts[hd, pl.ds(i, 1), :], (_SUBLANES, T))
                cp = jnp.broadcast_to(cps[hd, pl.ds(i, 1), :], (_SUBLANES, T))
                cnt = _pack_rows(cnt, cnt)
                cp = _pack_rows(cp, cp)
                for gq in range(nk // _PK):
                    rows = slice(gq * _PK, (gq + 1) * _PK)
                    e1 = e1s[hd, rows, :]
                    w[gq] = w[gq] + jnp.where(r1s[hd, rows, :] < cnt, e1, jnp.zeros_like(e1)) * cp
            for gq in range(nk // _PK):
                wbuf[sb, ii * nk + gq * _PK:ii * nk + (gq + 1) * _PK, :] = w[gq]
    for sb in range(n_sub):
        st[sb] = jnp.dot(u_ref[sb * ehalf:(sb + 1) * ehalf, :], ht[...], preferred_element_type=jnp.float32)
    for sb in range(n_sub):
        for gq in range(ehalf // _PK):
            rows = slice(gq * _PK, (gq + 1) * _PK)
            z = st[sb, rows, :] * (2.0 ** -0.5)
            g = z * (1.0 + lax.erf(z))
            at[sb, rows, :] = _pack_rows(g[0:_SUBLANES, :], g[_SUBLANES:_PK, :]) * wbuf[sb, rows, :]
    out = jnp.dot(vt_ref[:, 0:ehalf], at[0], preferred_element_type=jnp.float32)
    for sb in range(1, n_sub):
        out = out + jnp.dot(vt_ref[:, sb * ehalf:(sb + 1) * ehalf], at[sb], preferred_element_type=jnp.float32)
    acc[...] += out

    @pl.when(j == pl.num_programs(1) - 1)
    def _finish():
        y = x_ref[...] + acc[...].T
        o_ref[...] = _rms_rows(y, fg_ref[...])


def _peer_final(x, g, w_pq, subkeys, u, v, final_g, *, heads, topk, tile, etile):
    N, D = x.shape
    ne = u.shape[0]
    nk, half = subkeys.shape[1], subkeys.shape[2]
    T = min(tile, N)
    assert N % T == 0 and T % _LANES == 0 and ne == nk * nk and ne % etile == 0 and etile % nk == 0
    assert nk == topk * _SUBLANES and w_pq.shape[1] == heads * 2 * half
    ehalf = min(etile, 2 * nk)
    assert ehalf % nk == 0 and etile % ehalf == 0
    bf = jnp.bfloat16
    f32 = jnp.float32
    n_et = ne // etile
    vt = v.astype(bf).reshape(n_et, etile, D).transpose(0, 2, 1)
    xspec = pl.BlockSpec((T, D), lambda i, j: (i, 0))
    return pl.pallas_call(
        functools.partial(_peer_kernel, heads=heads, topk=topk, ehalf=ehalf),
        out_shape=jax.ShapeDtypeStruct((N, D), f32),
        grid=(N // T, n_et),
        in_specs=[xspec, _const_spec((1, D)), _const_spec((w_pq.shape[1], D)), _const_spec(subkeys.shape),
                  pl.BlockSpec((etile, D), lambda i, j: (j, 0)),
                  pl.BlockSpec((None, D, etile), lambda i, j: (j, 0, 0)),
                  _const_spec((1, D))],
        out_specs=xspec,
        scratch_shapes=[
            pltpu.VMEM((D, T), bf),
            pltpu.VMEM((heads, nk, T), f32),
            pltpu.VMEM((heads, nk, T), f32),
            pltpu.VMEM((heads, nk, T), bf),
            pltpu.VMEM((heads, nk, T), bf),
            pltpu.VMEM((D, T), f32),
            pltpu.VMEM((etile // ehalf, ehalf, T), f32),
            pltpu.VMEM((etile // ehalf, ehalf, T), bf),
            pltpu.VMEM((etile // ehalf, ehalf, T), bf),
        ],
        compiler_params=pltpu.CompilerParams(
            dimension_semantics=("arbitrary", "arbitrary"), vmem_limit_bytes=_VMEM_LIMIT),
        name="peer_final",
    )(x, g.reshape(1, D), w_pq.T.astype(bf), subkeys.astype(bf), u.astype(bf), vt, final_g.reshape(1, D))


def kernel(x, mem, norm_mix_g, w_in, cf_b_pw1, sc_conv_w, sc_w_out, cf_conv_w, cf_conv_b, cf_ln_g, cf_ln_b, cf_w_pw2, cf_b_pw2, w_mix_out, norm_xa_g, norm_mem_g, w_q, w_kv, w_xo, norm_peer_g, w_peer_q, peer_subkeys, peer_u, peer_v, final_norm_g):
    B, S, D = x.shape
    depth = norm_mix_g.shape[0]
    heads_xa = 4
    heads_peer = 8
    topk = 16
    for l in range(depth):
        x = _conv_mixers(x, norm_mix_g[l], w_in[l], cf_b_pw1[l], sc_conv_w[l], sc_w_out[l], cf_conv_w[l], cf_conv_b[l],
                         cf_ln_g[l], cf_ln_b[l], cf_w_pw2[l], cf_b_pw2[l], w_mix_out[l], tile=512)
        k, v = _mem_kv(mem, norm_mem_g[l], w_kv[l])
        x = _cross_attention(x, norm_xa_g[l], w_q[l], k, v, w_xo[l], heads=heads_xa, tile=512)
        last = l == depth - 1
        fg = final_norm_g if last else None
        assert last, "final RMSNorm is fused into the last PEER call"
        x = _peer_final(x.reshape(B * S, D), norm_peer_g[l], w_peer_q[l], peer_subkeys[l], peer_u[l], peer_v[l], fg,
                        heads=heads_peer, topk=topk, tile=512, etile=2048).reshape(B, S, D)
    return x
```

```python
import functools

import jax
import jax.numpy as jnp
from jax import lax
from jax.experimental import pallas as pl
from jax.experimental.pallas import tpu as pltpu

_EPS = 1e-6
_LANES = 128
_SUBLANES = 8
_CF_HALO = 32
_SC_HALO = 8
_ROWS = 32
_CONV_ROWS = 64
_VMEM_LIMIT = 56 * 1024 * 1024


def _row_loop(n_rows, chunk, body, unroll=1):
    def step(i, carry):
        body(pl.multiple_of(i * chunk, chunk))
        return carry
    lax.fori_loop(0, n_rows // chunk, step, 0, unroll=unroll)


def _rms_rows(x, g):
    ms = jnp.mean(x * x, axis=-1, keepdims=True)
    return x * lax.rsqrt(ms + _EPS) * g


def _const_spec(shape):
    nd = len(shape)
    return pl.BlockSpec(shape, lambda *_: (0,) * nd, pipeline_mode=pl.Buffered(1))


def _mix_kernel(x_ref, g_ref, win_ref, b1_ref, scw_ref, scwo_ref, cfw_ref, cfcb_ref, lng_ref, lnb_ref,
                pw2_ref, b2_ref, wmo_ref, o_ref, hb, t0, t1, cfbuf, cfout, scbuf, act, macc, *, sc_k, cf_k):
    T, D = x_ref.shape
    nlb = D // _LANES
    s = pl.program_id(1)

    @pl.when(s == 0)
    def _():
        cfbuf[:, 0:_CF_HALO, :] = jnp.zeros((nlb, _CF_HALO, _LANES), jnp.float32)
        scbuf[:, 0:_SC_HALO, :] = jnp.zeros((nlb, _SC_HALO, _LANES), jnp.float32)

    def norm_rows(r0):
        rows = pl.ds(r0, _ROWS)
        hb[rows, :] = _rms_rows(x_ref[rows, :], g_ref[...]).astype(jnp.bfloat16)
    _row_loop(T, _ROWS, norm_rows, unroll=2)

    def proj(col):
        return jnp.dot(hb[...], win_ref[:, col * D:(col + 1) * D], preferred_element_type=jnp.float32)

    t0[...] = proj(3)
    t1[...] = proj(4)

    def glu_rows(r0):
        rows = pl.ds(r0, _ROWS)
        a = t0[rows, :] + b1_ref[:, 0:D]
        g = t1[rows, :] + b1_ref[:, D:2 * D]
        v = a * jax.nn.sigmoid(g)
        for c in range(nlb):
            cfbuf[c, pl.ds(_CF_HALO + r0, _ROWS), :] = v[:, c * _LANES:(c + 1) * _LANES]
    _row_loop(T, _ROWS, glu_rows)

    def conv_rows(r0):
        for c in range(nlb):
            lanes = slice(c * _LANES, (c + 1) * _LANES)
            acc = jnp.broadcast_to(cfcb_ref[:, lanes], (_CONV_ROWS, _LANES))
            for k in range(cf_k):
                acc = acc + cfw_ref[k:k + 1, lanes] * cfbuf[c, pl.ds(r0 + _CF_HALO - (cf_k - 1) + k, _CONV_ROWS), :]
            cfout[c, pl.ds(r0, _CONV_ROWS), :] = acc
    _row_loop(T, _CONV_ROWS, conv_rows)

    def cf_rows(r0):
        rows = pl.ds(r0, _ROWS)
        accs = [cfout[c, rows, :] for c in range(nlb)]
        tot = accs[0]
        for c in range(1, nlb):
            tot = tot + accs[c]
        mu = jnp.sum(tot, axis=-1, keepdims=True) * (1.0 / D)
        devs = [a - mu for a in accs]
        sq = devs[0] * devs[0]
        for c in range(1, nlb):
            sq = sq + devs[c] * devs[c]
        rstd = lax.rsqrt(jnp.sum(sq, axis=-1, keepdims=True) * (1.0 / D) + _EPS)
        for c in range(nlb):
            lanes = slice(c * _LANES, (c + 1) * _LANES)
            y = devs[c] * rstd * lng_ref[:, lanes] + lnb_ref[:, lanes]
            act[rows, lanes] = (y * jax.nn.sigmoid(y)).astype(jnp.bfloat16)
    _row_loop(T, _ROWS, cf_rows, unroll=2)

    t0[...] = jnp.dot(act[...], pw2_ref[...], preferred_element_type=jnp.float32)
    t1[...] = proj(6)

    def gate_cf_rows(r0):
        rows = pl.ds(r0, _ROWS)
        macc[rows, :] = jax.nn.sigmoid(t1[rows, :]) * (t0[rows, :] + b2_ref[...])
    _row_loop(T, _ROWS, gate_cf_rows)

    t0[...] = proj(1)
    t1[...] = proj(2)

    def cx_rows(r0):
        rows = pl.ds(r0, _ROWS)
        v = t0[rows, :] * t1[rows, :]
        for c in range(nlb):
            scbuf[c, pl.ds(_SC_HALO + r0, _ROWS), :] = v[:, c * _LANES:(c + 1) * _LANES]
    _row_loop(T, _ROWS, cx_rows)

    t0[...] = proj(0)

    def sc_rows(r0):
        rows = pl.ds(r0, _ROWS)
        for c in range(nlb):
            lanes = slice(c * _LANES, (c + 1) * _LANES)
            acc = scw_ref[0:1, lanes] * scbuf[c, pl.ds(r0 + _SC_HALO - (sc_k - 1), _ROWS), :]
            for k in range(1, sc_k):
                acc = acc + scw_ref[k:k + 1, lanes] * scbuf[c, pl.ds(r0 + _SC_HALO - (sc_k - 1) + k, _ROWS), :]
            act[rows, lanes] = (t0[rows, lanes] * acc).astype(jnp.bfloat16)
    _row_loop(T, _ROWS, sc_rows)

    t0[...] = jnp.dot(act[...], scwo_ref[...], preferred_element_type=jnp.float32)
    t1[...] = proj(5)

    def merge_rows(r0):
        rows = pl.ds(r0, _ROWS)
        m = macc[rows, :] + jax.nn.sigmoid(t1[rows, :]) * t0[rows, :]
        act[rows, :] = m.astype(jnp.bfloat16)
    _row_loop(T, _ROWS, merge_rows)

    o_ref[...] = x_ref[...] + jnp.dot(act[...], wmo_ref[...], preferred_element_type=jnp.float32)

    cfbuf[:, 0:_CF_HALO, :] = cfbuf[:, T:T + _CF_HALO, :]
    scbuf[:, 0:_SC_HALO, :] = scbuf[:, T:T + _SC_HALO, :]


def _conv_mixers(x, g, w_in, b1, sc_w, sc_wo, cf_w, cf_cb, ln_g, ln_b, pw2, b2, w_mo, *, tile):
    B, S, D = x.shape
    T = min(tile, S)
    assert S % T == 0 and T % _ROWS == 0 and T % _CONV_ROWS == 0 and D % _LANES == 0 and T >= _CF_HALO
    sc_k, cf_k = sc_w.shape[0], cf_w.shape[0]
    assert cf_k - 1 <= _CF_HALO and sc_k - 1 <= _SC_HALO
    nlb = D // _LANES
    bf = jnp.bfloat16
    row = lambda v: v.reshape(1, -1)
    kern = functools.partial(_mix_kernel, sc_k=sc_k, cf_k=cf_k)
    xspec = pl.BlockSpec((None, T, D), lambda b, s: (b, s, 0))
    args = (x, row(g), w_in.astype(bf), row(b1), sc_w, sc_wo.astype(bf), cf_w, row(cf_cb), row(ln_g), row(ln_b),
            pw2.astype(bf), row(b2), w_mo.astype(bf))
    return pl.pallas_call(
        kern,
        out_shape=jax.ShapeDtypeStruct((B, S, D), jnp.float32),
        grid=(B, S // T),
        in_specs=[xspec] + [_const_spec(a.shape) for a in args[1:]],
        out_specs=xspec,
        scratch_shapes=[
            pltpu.VMEM((T, D), bf),
            pltpu.VMEM((T, D), jnp.float32),
            pltpu.VMEM((T, D), jnp.float32),
            pltpu.VMEM((nlb, _CF_HALO + T, _LANES), jnp.float32),
            pltpu.VMEM((nlb, T, _LANES), jnp.float32),
            pltpu.VMEM((nlb, _SC_HALO + T, _LANES), jnp.float32),
            pltpu.VMEM((T, D), bf),
            pltpu.VMEM((T, D), jnp.float32),
        ],
        compiler_params=pltpu.CompilerParams(
            dimension_semantics=("arbitrary", "arbitrary"), vmem_limit_bytes=_VMEM_LIMIT),
        name="conv_mixers",
    )(*args)


def _memkv_kernel(mem_ref, g_ref, wkv_ref, k_ref, v_ref):
    E = k_ref.shape[-1]
    mn = _rms_rows(mem_ref[...], g_ref[...]).astype(jnp.bfloat16)
    kv = jnp.dot(mn, wkv_ref[...], preferred_element_type=jnp.float32)
    k_ref[...] = kv[:, 0:E].astype(jnp.bfloat16)
    v_ref[...] = kv[:, E:2 * E].astype(jnp.bfloat16)


def _mem_kv(mem, g, w_kv):
    B, M, D = mem.shape
    E = w_kv.shape[1] // 2
    bf = jnp.bfloat16
    out = jax.ShapeDtypeStruct((B, M, E), bf)
    ospec = pl.BlockSpec((None, M, E), lambda b: (b, 0, 0))
    return pl.pallas_call(
        _memkv_kernel,
        out_shape=(out, out),
        grid=(B,),
        in_specs=[pl.BlockSpec((None, M, D), lambda b: (b, 0, 0)), _const_spec((1, D)), _const_spec(w_kv.shape)],
        out_specs=(ospec, ospec),
        compiler_params=pltpu.CompilerParams(dimension_semantics=("arbitrary",), vmem_limit_bytes=_VMEM_LIMIT),
        name="mem_kv",
    )(mem, g.reshape(1, D), w_kv.astype(bf))


def _xattn_kernel(x_ref, g_ref, wq_ref, k_ref, v_ref, wo_ref, o_ref, hb, ob, *, heads):
    T, D = x_ref.shape
    E = wq_ref.shape[1]
    hd = E // heads

    def norm_rows(r0):
        rows = pl.ds(r0, _ROWS)
        hb[rows, :] = _rms_rows(x_ref[rows, :], g_ref[...]).astype(jnp.bfloat16)
    _row_loop(T, _ROWS, norm_rows, unroll=2)

    q = jnp.dot(hb[...], wq_ref[...], preferred_element_type=jnp.float32).astype(jnp.bfloat16)
    for h in range(heads):
        cols = slice(h * hd, (h + 1) * hd)
        sc = lax.dot_general(q[:, cols], k_ref[:, cols], (((1,), (1,)), ((), ())),
                             preferred_element_type=jnp.float32) * (hd ** -0.5)
        sc = sc - jnp.max(sc, axis=-1, keepdims=True)
        p = jnp.exp(sc)
        p = p / jnp.sum(p, axis=-1, keepdims=True)
        ob[:, cols] = jnp.dot(p.astype(jnp.bfloat16), v_ref[:, cols],
                              preferred_element_type=jnp.float32).astype(jnp.bfloat16)
    o_ref[...] = x_ref[...] + jnp.dot(ob[...], wo_ref[...], preferred_element_type=jnp.float32)


def _cross_attention(x, g, w_q, k, v, w_o, *, heads, tile):
    B, S, D = x.shape
    M, E = k.shape[1], k.shape[2]
    T = min(tile, S)
    assert S % T == 0 and T % _ROWS == 0
    bf = jnp.bfloat16
    xspec = pl.BlockSpec((None, T, D), lambda b, s: (b, s, 0))
    kvspec = pl.BlockSpec((None, M, E), lambda b, s: (b, 0, 0))
    return pl.pallas_call(
        functools.partial(_xattn_kernel, heads=heads),
        out_shape=jax.ShapeDtypeStruct((B, S, D), jnp.float32),
        grid=(B, S // T),
        in_specs=[xspec, _const_spec((1, D)), _const_spec(w_q.shape), kvspec, kvspec, _const_spec(w_o.shape)],
        out_specs=xspec,
        scratch_shapes=[pltpu.VMEM((T, D), bf), pltpu.VMEM((T, E), bf)],
        compiler_params=pltpu.CompilerParams(
            dimension_semantics=("arbitrary", "arbitrary"), vmem_limit_bytes=_VMEM_LIMIT),
        name="mem_cross_attention",
    )(x, g.reshape(1, D), w_q.astype(bf), k, v, w_o.astype(bf))


def _batcher_pairs(n):
    pairs = []
    p = 1
    while p < n:
        k = p
        while k >= 1:
            for j in range(k % p, n - k, 2 * k):
                for i in range(min(k, n - j - k)):
                    if (i + j) // (2 * p) == (i + j + k) // (2 * p):
                        pairs.append((i + j, i + j + k))
            k //= 2
        p *= 2
    return pairs


def _sort_desc(xs):
    xs = list(xs)
    for i, j in _batcher_pairs(len(xs)):
        hi, lo = jnp.maximum(xs[i], xs[j]), jnp.minimum(xs[i], xs[j])
        xs[i], xs[j] = hi, lo
    return xs


def _bitonic_sort_desc(xs):
    xs = list(xs)
    n = len(xs)
    stride = n // 2
    while stride >= 1:
        for i in range(n):
            if i & stride == 0:
                hi, lo = jnp.maximum(xs[i], xs[i + stride]), jnp.minimum(xs[i], xs[i + stride])
                xs[i], xs[i + stride] = hi, lo
        stride //= 2
    return xs


def _sublane_partner(xs, shift):
    return [pltpu.roll(x, shift, 0) for x in xs]


def _top_sorted(score, k):
    n = score.shape[0]
    assert n == k * _SUBLANES
    xs = _sort_desc([score[g * _SUBLANES:(g + 1) * _SUBLANES, :] for g in range(k)])
    for shift in (4, 2, 1):
        ps = _sublane_partner(xs, shift)
        xs = _bitonic_sort_desc([jnp.maximum(xs[r], ps[k - 1 - r]) for r in range(k)])
    return xs


def _pair_threshold(a, b, k):
    assert k == 16
    neg = jnp.float32(-jnp.inf)
    sub = lax.broadcasted_iota(jnp.int32, a[0].shape, 0)

    def stack(vals):
        out = vals[-1]
        for s in range(len(vals) - 2, -1, -1):
            out = jnp.where(sub == s, vals[s], out)
        return out

    a_lo, a_hi = stack(a[0:8]), stack(a[8:16])
    b_lo, b_hi = stack(b[0:8]), stack(b[8:16])
    mid = sub >= 2
    cands = [
        a[0] + b_lo, a[0] + b_hi, a[1] + b_lo, a_hi + b[0],
        jnp.where(mid, a_lo + b[0], neg), jnp.where(mid, a_lo + b[1], neg),
        jnp.where(sub < 6,
                  stack([a[2], a[2], a[2], a[3], a[3], a[4], a[4], a[4]])
                  + stack([b[2], b[3], b[4], b[2], b[3], b[2], b[2], b[2]]), neg),
        jnp.full(a[0].shape, neg, jnp.float32),
    ]
    xs = _sort_desc(cands)
    ps = _sublane_partner(xs, 4)
    hi = _bitonic_sort_desc([jnp.maximum(xs[r], ps[7 - r]) for r in range(8)])
    lo = _bitonic_sort_desc([jnp.minimum(xs[r], ps[7 - r]) for r in range(8)])
    xs = hi + lo
    ps = _sublane_partner(xs, 2)
    xs = _bitonic_sort_desc([jnp.maximum(xs[r], ps[k - 1 - r]) for r in range(k)])
    ps = _sublane_partner(xs, 1)
    tau = jnp.maximum(xs[0], ps[k - 1])
    for r in range(1, k):
        tau = jnp.minimum(tau, jnp.maximum(xs[r], ps[k - 1 - r]))
    top = a[0] + b[0]
    z = jnp.zeros_like(tau)
    for c in cands[:-1]:
        z = z + jnp.where(c >= tau, jnp.exp(c - top), 0.0)
    z = jnp.sum(z, axis=0, keepdims=True)
    return tau, jnp.broadcast_to(z, tau.shape)


_PK = 16


def _pack_rows(lo, hi):
    return jnp.concatenate([lo, hi], axis=0).astype(jnp.bfloat16)


def _rank_among(s, b):
    r = jnp.full(s.shape, float(len(b)), jnp.float32)
    for k in range(len(b) - 1, -1, -1):
        r = jnp.where(s >= b[k], float(k), r)
    return r


def _pair_count(s, b, tau):
    c = jnp.zeros(s.shape, jnp.float32)
    for k in range(len(b)):
        c = jnp.where(s + b[k] >= tau, float(k + 1), c)
    return c


def _peer_kernel(x_ref, g_ref, wq_ref, sk_ref, u_ref, vt_ref, fg_ref, o_ref,
                 ht, cnts, cps, r1s, e1s, acc, st, wbuf, at, *, heads, topk, ehalf):
    T, D = x_ref.shape
    et = u_ref.shape[0]
    nk = sk_ref.shape[1]
    half = sk_ref.shape[2]
    j = pl.program_id(1)
    bf = jnp.bfloat16

    @pl.when(j == 0)
    def _route():
        h = _rms_rows(x_ref[...], g_ref[...])
        ht[...] = h.T.astype(bf)
        acc[...] = jnp.zeros_like(acc)
        for hd in range(heads):
            q = jnp.dot(wq_ref[hd * 2 * half:(hd + 1) * 2 * half, :], ht[...],
                        preferred_element_type=jnp.float32).astype(bf)
            s0 = jnp.dot(sk_ref[0], q[0:half, :], preferred_element_type=jnp.float32)
            s1 = jnp.dot(sk_ref[1], q[half:2 * half, :], preferred_element_type=jnp.float32)
            a = _top_sorted(s0, topk)
            b = _top_sorted(s1, topk)
            tau, z = _pair_threshold(a, b, topk)
            for gq in range(nk // _SUBLANES):
                rows = slice(gq * _SUBLANES, (gq + 1) * _SUBLANES)
                cnts[hd, rows, :] = _pair_count(s0[rows, :], b, tau)
                cps[hd, rows, :] = jnp.exp(s0[rows, :] - a[0]) * ((2.0 ** -0.5) / z)
            for gq in range(nk // _PK):
                rows = slice(gq * _PK, (gq + 1) * _PK)
                lo = s1[gq * _PK:gq * _PK + _SUBLANES, :]
                hi = s1[gq * _PK + _SUBLANES:(gq + 1) * _PK, :]
                r1s[hd, rows, :] = _pack_rows(_rank_among(lo, b), _rank_among(hi, b))
                e1s[hd, rows, :] = _pack_rows(jnp.exp(lo - b[0]), jnp.exp(hi - b[0]))

    n_sub = et // ehalf
    per = ehalf // nk
    for sb in range(n_sub):
        for ii in range(per):
            i = (j * n_sub + sb) * per + ii
            w = [jnp.zeros((_PK, T), bf) for _ in range(nk // _PK)]
            for hd in range(heads):
                cnt = jnp.broadcast_to(cnts[hd, pl.ds(i, 1), :], (_SUBLANES, T))
                cp = jnp.broadcast_to(cps[hd, pl.ds(i, 1), :], (_SUBLANES, T))
                cnt = _pack_rows(cnt, cnt)
                cp = _pack_rows(cp, cp)
                for gq in range(nk // _PK):
                    rows = slice(gq * _PK, (gq + 1) * _PK)
                    e1 = e1s[hd, rows, :]
                    w[gq] = w[gq] + jnp.where(r1s[hd, rows, :] < cnt, e1, jnp.zeros_like(e1)) * cp
            for gq in range(nk // _PK):
                wbuf[sb, ii * nk + gq * _PK:ii * nk + (gq + 1) * _PK, :] = w[gq]
    for sb in range(n_sub):
        st[sb] = jnp.dot(u_ref[sb * ehalf:(sb + 1) * ehalf, :], ht[...], preferred_element_type=jnp.float32)
    for sb in range(n_sub):
        for gq in range(ehalf // _PK):
            rows = slice(gq * _PK, (gq + 1) * _PK)
            z = st[sb, rows, :] * (2.0 ** -0.5)
            g = z * (1.0 + lax.erf(z))
            at[sb, rows, :] = _pack_rows(g[0:_SUBLANES, :], g[_SUBLANES:_PK, :]) * wbuf[sb, rows, :]
    out = jnp.dot(vt_ref[:, 0:ehalf], at[0], preferred_element_type=jnp.float32)
    for sb in range(1, n_sub):
        out = out + jnp.dot(vt_ref[:, sb * ehalf:(sb + 1) * ehalf], at[sb], preferred_element_type=jnp.float32)
    acc[...] += out

    @pl.when(j == pl.num_programs(1) - 1)
    def _finish():
        y = x_ref[...] + acc[...].T
        o_ref[...] = _rms_rows(y, fg_ref[...])


def _peer_final(x, g, w_pq, subkeys, u, v, final_g, *, heads, topk, tile, etile):
    N, D = x.shape
    ne = u.shape[0]
    nk, half = subkeys.shape[1], subkeys.shape[2]
    T = min(tile, N)
    assert N % T == 0 and T % _LANES == 0 and ne == nk * nk and ne % etile == 0 and etile % nk == 0
    assert nk == topk * _SUBLANES and w_pq.shape[1] == heads * 2 * half
    ehalf = min(etile, 2 * nk)
    assert ehalf % nk == 0 and etile % ehalf == 0
    bf = jnp.bfloat16
    f32 = jnp.float32
    n_et = ne // etile
    vt = v.astype(bf).reshape(n_et, etile, D).transpose(0, 2, 1)
    xspec = pl.BlockSpec((T, D), lambda i, j: (i, 0))
    return pl.pallas_call(
        functools.partial(_peer_kernel, heads=heads, topk=topk, ehalf=ehalf),
        out_shape=jax.ShapeDtypeStruct((N, D), f32),
        grid=(N // T, n_et),
        in_specs=[xspec, _const_spec((1, D)), _const_spec((w_pq.shape[1], D)), _const_spec(subkeys.shape),
                  pl.BlockSpec((etile, D), lambda i, j: (j, 0)),
                  pl.BlockSpec((None, D, etile), lambda i, j: (j, 0, 0)),
                  _const_spec((1, D))],
        out_specs=xspec,
        scratch_shapes=[
            pltpu.VMEM((D, T), bf),
            pltpu.VMEM((heads, nk, T), f32),
            pltpu.VMEM((heads, nk, T), f32),
            pltpu.VMEM((heads, nk, T), bf),
            pltpu.VMEM((heads, nk, T), bf),
            pltpu.VMEM((D, T), f32),
            pltpu.VMEM((etile // ehalf, ehalf, T), f32),
            pltpu.VMEM((etile // ehalf, ehalf, T), bf),
            pltpu.VMEM((etile // ehalf, ehalf, T), bf),
        ],
        compiler_params=pltpu.CompilerParams(
            dimension_semantics=("arbitrary", "arbitrary"), vmem_limit_bytes=_VMEM_LIMIT),
        name="peer_final",
    )(x, g.reshape(1, D), w_pq.T.astype(bf), subkeys.astype(bf), u.astype(bf), vt, final_g.reshape(1, D))


def kernel(x, mem, norm_mix_g, w_in, cf_b_pw1, sc_conv_w, sc_w_out, cf_conv_w, cf_conv_b, cf_ln_g, cf_ln_b, cf_w_pw2, cf_b_pw2, w_mix_out, norm_xa_g, norm_mem_g, w_q, w_kv, w_xo, norm_peer_g, w_peer_q, peer_subkeys, peer_u, peer_v, final_norm_g):
    B, S, D = x.shape
    depth = norm_mix_g.shape[0]
    heads_xa = 4
    heads_peer = 8
    topk = 16
    for l in range(depth):
        x = _conv_mixers(x, norm_mix_g[l], w_in[l], cf_b_pw1[l], sc_conv_w[l], sc_w_out[l], cf_conv_w[l], cf_conv_b[l],
                         cf_ln_g[l], cf_ln_b[l], cf_w_pw2[l], cf_b_pw2[l], w_mix_out[l], tile=512)
        k, v = _mem_kv(mem, norm_mem_g[l], w_kv[l])
        x = _cross_attention(x, norm_xa_g[l], w_q[l], k, v, w_xo[l], heads=heads_xa, tile=1024)
        last = l == depth - 1
        fg = final_norm_g if last else None
        assert last, "final RMSNorm is fused into the last PEER call"
        x = _peer_final(x.reshape(B * S, D), norm_peer_g[l], w_peer_q[l], peer_subkeys[l], peer_u[l], peer_v[l], fg,
                        heads=heads_peer, topk=topk, tile=512, etile=2048).reshape(B, S, D)
    return x
```
